```python
import jax, jax.numpy as jnp
from jax import lax
import numpy as np

D_MODEL = 2048
BATCH = 2
SEQ = 8192
DEPTH = 1

MIX_WIDTH = D_MODEL
CONV_WIDTH = MIX_WIDTH // 2
RWKV_WIDTH = MIX_WIDTH - CONV_WIDTH
HEAD_DIM = 64
N_RWKV_HEADS = RWKV_WIDTH // HEAD_DIM
N_CONV_GROUPS = CONV_WIDTH // HEAD_DIM
CONV_KERNEL = 31
DECAY_LORA = 96
ICLR_LORA = 96
GATE_LORA = 256
D_FF = ((8 * D_MODEL + 767) // 768) * 256
RMS_EPS = 1e-6
CONV_GN_EPS = 1e-5
RWKV_GN_EPS = 64e-5
N_MOD = 6

CONV_COLS = 2 * CONV_WIDTH
RWKV_COLS = 3 * RWKV_WIDTH + DECAY_LORA + ICLR_LORA + GATE_LORA
IN_COLS = CONV_COLS + RWKV_COLS

kernel_name = "hymba_style_conv_rwkv7_adaln_block"


def rmsnorm(x, g):
    xf = x.astype(jnp.float32)
    y = xf * lax.rsqrt(jnp.mean(xf * xf, axis=-1, keepdims=True) + RMS_EPS)
    return (y * g).astype(x.dtype)


def group_norm(x, n_groups, gain, bias, eps):
    shp = x.shape
    xf = x.astype(jnp.float32).reshape(shp[:-1] + (n_groups, shp[-1] // n_groups))
    mean = jnp.mean(xf, axis=-1, keepdims=True)
    var = jnp.mean(jnp.square(xf - mean), axis=-1, keepdims=True)
    y = ((xf - mean) * lax.rsqrt(var + eps)).reshape(shp)
    return (y * gain + bias).astype(x.dtype)


def token_shift(p, mu):
    prev = jnp.pad(p, ((0, 0), (1, 0), (0, 0)))[:, :-1]
    return p + (prev - p) * mu


def causal_depthwise_conv(u, w, b):
    C = u.shape[-1]
    y = lax.conv_general_dilated(u, w[:, None, :].astype(u.dtype), window_strides=(1,),
                                 padding=[(w.shape[0] - 1, 0)],
                                 dimension_numbers=('NWC', 'WIO', 'NWC'),
                                 feature_group_count=C)
    return y + b


def wkv7_scan(r, decay, k, v, a, b):
    Bn, T, H, N = r.shape
    xs = tuple(jnp.moveaxis(t.astype(jnp.float32), 1, 0) for t in (r, decay, k, v, a, b))

    def step(S, inp):
        r_t, w_t, k_t, v_t, a_t, b_t = inp
        sa = jnp.einsum('bhij,bhj->bhi', S, a_t)
        S = S * w_t[:, :, None, :] + sa[..., None] * b_t[:, :, None, :] + v_t[..., None] * k_t[:, :, None, :]
        y = jnp.einsum('bhij,bhj->bhi', S, r_t)
        return S, y

    S0 = jnp.zeros((Bn, H, N, N), jnp.float32)
    _, ys = lax.scan(step, S0, xs)
    return jnp.moveaxis(ys, 0, 1)


def hybrid_mixer(h, w_in, tshift_mu, conv_w, conv_b, conv_gn_g, conv_gn_b,
                 w0, w_w2, a0, a_w2, g_w2, k_k, k_a, r_k, lnx_g, lnx_b, w_out):
    Bn, T, _ = h.shape
    p = h @ w_in
    p_conv, p_rwkv = p[..., :CONV_COLS], p[..., CONV_COLS:]

    val, gate = p_conv[..., :CONV_WIDTH], p_conv[..., CONV_WIDTH:]
    u = val * jax.nn.sigmoid(gate)
    u = causal_depthwise_conv(u, conv_w, conv_b)
    u = group_norm(u, N_CONV_GROUPS, conv_gn_g, conv_gn_b, CONV_GN_EPS)
    out_conv = jax.nn.silu(u)

    q = token_shift(p_rwkv, tshift_mu)
    o1, o2, o3 = RWKV_WIDTH, 2 * RWKV_WIDTH, 3 * RWKV_WIDTH
    o4, o5 = o3 + DECAY_LORA, o3 + DECAY_LORA + ICLR_LORA
    r, k, v = q[..., :o1], q[..., o1:o2], q[..., o2:o3]
    xw, xa, xg = q[..., o3:o4], q[..., o4:o5], q[..., o5:]

    w_log = -jax.nn.softplus(-(w0 + jnp.tanh(xw) @ w_w2)) - 0.5
    decay = jnp.exp(-jnp.exp(w_log.astype(jnp.float32)))
    a = jax.nn.sigmoid(a0 + xa @ a_w2)
    g = jax.nn.sigmoid(xg) @ g_w2

    hs = (Bn, T, N_RWKV_HEADS, HEAD_DIM)
    kk = (k * k_k).reshape(hs).astype(jnp.float32)
    kk = kk / jnp.maximum(jnp.linalg.norm(kk, axis=-1, keepdims=True), 1e-12)
    k = k * (1.0 + (a - 1.0) * k_a)
    rh, kh, vh = r.reshape(hs), k.reshape(hs), v.reshape(hs)
    ah = a.reshape(hs).astype(jnp.float32)
    y = wkv7_scan(rh, decay.reshape(hs), kh, vh, -kk, kk * ah)
    y = group_norm(y.reshape(Bn, T, RWKV_WIDTH), N_RWKV_HEADS, lnx_g, lnx_b, RWKV_GN_EPS)
    bonus = jnp.sum(rh.astype(jnp.float32) * kh * r_k, axis=-1, keepdims=True) * vh
    out_rwkv = ((y + bonus.reshape(Bn, T, RWKV_WIDTH)) * g).astype(h.dtype)

    return jnp.concatenate([out_conv.astype(h.dtype), out_rwkv], axis=-1) @ w_out


def swiglu(h, w1, w3, w2):
    return (jax.nn.silu(h @ w1) * (h @ w3)) @ w2


def setup_inputs(seed: int = 0) -> dict:
    key = jax.random.key(seed)
    ks = jax.random.split(key, 32)
    L = DEPTH
    f32 = jnp.float32

    def nrm(k, shape, fan_in, scale=1.0):
        return jax.random.normal(k, shape, f32) * (scale * fan_in ** -0.5)

    def small(k, shape, s=0.02):
        return jax.random.normal(k, shape, f32) * s

    return {
        "x": jax.random.normal(ks[0], (BATCH, SEQ, D_MODEL), f32),
        "c": jax.random.normal(ks[1], (BATCH, D_MODEL), f32),
        "w_ada": nrm(ks[2], (L, D_MODEL, N_MOD * D_MODEL), D_MODEL, 0.5),
        "b_ada": small(ks[3], (L, N_MOD * D_MODEL)),
        "norm1_g": 1.0 + small(ks[4], (L, D_MODEL)),
        "w_in": nrm(ks[5], (L, D_MODEL, IN_COLS), D_MODEL),
        "tshift_mu": jax.random.uniform(ks[6], (L, RWKV_COLS), f32),
        "conv_w": nrm(ks[7], (L, CONV_KERNEL, CONV_WIDTH), CONV_KERNEL),
        "conv_b": small(ks[8], (L, CONV_WIDTH)),
        "conv_gn_g": 1.0 + small(ks[9], (L, CONV_WIDTH)),
        "conv_gn_b": small(ks[10], (L, CONV_WIDTH)),
        "w0": jax.random.uniform(ks[11], (L, RWKV_WIDTH), f32, -5.0, 1.0),
        "w_w2": nrm(ks[12], (L, DECAY_LORA, RWKV_WIDTH), DECAY_LORA, 0.5),
        "a0": small(ks[13], (L, RWKV_WIDTH), 0.1),
        "a_w2": nrm(ks[14], (L, ICLR_LORA, RWKV_WIDTH), ICLR_LORA, 0.5),
        "g_w2": nrm(ks[15], (L, GATE_LORA, RWKV_WIDTH), GATE_LORA),
        "k_k": 0.85 + small(ks[16], (L, RWKV_WIDTH), 0.05),
        "k_a": 1.0 + small(ks[17], (L, RWKV_WIDTH), 0.05),
        "r_k": small(ks[18], (L, N_RWKV_HEADS, HEAD_DIM), 0.1),
        "lnx_g": 1.0 + small(ks[19], (L, RWKV_WIDTH)),
        "lnx_b": small(ks[20], (L, RWKV_WIDTH)),
        "w_out": nrm(ks[21], (L, MIX_WIDTH, D_MODEL), MIX_WIDTH),
        "norm2_g": 1.0 + small(ks[22], (L, D_MODEL)),
        "w_ff1": nrm(ks[23], (L, D_MODEL, D_FF), D_MODEL),
        "w_ff3": nrm(ks[24], (L, D_MODEL, D_FF), D_MODEL),
        "w_ff2": nrm(ks[25], (L, D_FF, D_MODEL), D_FF),
        "norm_f_g": 1.0 + small(ks[26], (D_MODEL,)),
    }


def reference(x, c, w_ada, b_ada, norm1_g, w_in, tshift_mu, conv_w, conv_b, conv_gn_g,
              conv_gn_b, w0, w_w2, a0, a_w2, g_w2, k_k, k_a, r_k, lnx_g, lnx_b, w_out,
              norm2_g, w_ff1, w_ff3, w_ff2, norm_f_g):
    sc = jax.nn.silu(c)
    for l in range(DEPTH):
        mod = sc @ w_ada[l] + b_ada[l]
        sh1, sc1, g1, sh2, sc2, g2 = [m[:, None, :] for m in jnp.split(mod, N_MOD, axis=-1)]

        h = rmsnorm(x, norm1_g[l]) * (1.0 + sc1) + sh1
        mix = hybrid_mixer(h, w_in[l], tshift_mu[l], conv_w[l], conv_b[l], conv_gn_g[l],
                           conv_gn_b[l], w0[l], w_w2[l], a0[l], a_w2[l], g_w2[l], k_k[l],
                           k_a[l], r_k[l], lnx_g[l], lnx_b[l], w_out[l])
        x = x + g1 * mix

        h = rmsnorm(x, norm2_g[l]) * (1.0 + sc2) + sh2
        x = x + g2 * swiglu(h, w_ff1[l], w_ff3[l], w_ff2[l])
    return rmsnorm(x, norm_f_g)
```

```python
import functools
import math

import jax
import jax.numpy as jnp
from jax import lax
from jax.experimental import pallas as pl
from jax.experimental.pallas import tpu as pltpu

F32 = jnp.float32
BF16 = jnp.bfloat16

HEAD_DIM = 64
CONV_KERNEL = 31
RMS_EPS = 1e-6
CONV_GN_EPS = 1e-5
RWKV_GN_EPS = 64e-5
N_MOD = 6

LANES = 128
MXU_DIM = 256
GROUP = MXU_DIM
CHUNK = 64
HALO = 32
VMEM_LIMIT = 52 * 1024 * 1024


def _cparams(sem):
    return pltpu.CompilerParams(dimension_semantics=sem, vmem_limit_bytes=VMEM_LIMIT)


def _dot(a, b):
    return jnp.dot(a, b, preferred_element_type=F32)


def _dot_nt(a, b):
    return lax.dot_general(a, b, (((1,), (1,)), ((), ())), preferred_element_type=F32)


def _dot_tn(a, b):
    return lax.dot_general(a, b, (((0,), (0,)), ((), ())), preferred_element_type=F32)


def _split2(x):
    hi = x.astype(BF16)
    lo = (x - hi.astype(F32)).astype(BF16)
    return hi, lo


def _split3(x):
    hi = x.astype(BF16)
    r1 = x - hi.astype(F32)
    mid = r1.astype(BF16)
    lo = (r1 - mid.astype(F32)).astype(BF16)
    return hi, mid, lo


def _sigmoid(x):
    return 1.0 / (1.0 + jnp.exp(-x))


def _head_sum(x, ones_bd):
    outs = []
    for g in range(x.shape[1] // GROUP):
        xs = x[:, g * GROUP:(g + 1) * GROUP]
        hi, lo = _split2(xs)
        outs.append(_dot(hi, ones_bd) + _dot(lo, ones_bd))
    return outs[0] if len(outs) == 1 else jnp.concatenate(outs, axis=1)


def _block_diag_ones():
    r = lax.broadcasted_iota(jnp.int32, (GROUP, GROUP), 0) // HEAD_DIM
    c = lax.broadcasted_iota(jnp.int32, (GROUP, GROUP), 1) // HEAD_DIM
    return jnp.where(r == c, 1.0, 0.0).astype(BF16)


def _ada_kernel(cb_ref, w_ref, b_ref, o_ref):
    nb = cb_ref.shape[0]
    tn = w_ref.shape[1]
    for b in range(nb):
        cb = cb_ref[b]
        sc = cb * _sigmoid(cb)
        for j in range(tn // LANES):
            sl = slice(j * LANES, (j + 1) * LANES)
            s = jnp.sum(w_ref[:, sl] * sc, axis=0, keepdims=True)
            o_ref[b:b + 1, sl] = s + b_ref[:, sl]


def _ada(c, w_ada, b_ada):
    nb, d = c.shape
    n = w_ada.shape[1]
    tn = 1024
    cb = jnp.broadcast_to(c[:, :, None], (nb, d, LANES))
    return pl.pallas_call(
        _ada_kernel,
        grid=(n // tn,),
        in_specs=[pl.BlockSpec((nb, d, LANES), lambda j: (0, 0, 0)),
                  pl.BlockSpec((d, tn), lambda j: (0, j)),
                  pl.BlockSpec((1, tn), lambda j: (0, j))],
        out_specs=pl.BlockSpec((nb, tn), lambda j: (0, j)),
        out_shape=jax.ShapeDtypeStruct((nb, n), F32),
        compiler_params=_cparams(("arbitrary",)),
        name="ada",
    )(cb, w_ada, b_ada.reshape(1, n))


def _modulated_norm(x, g, sc, sh):
    ms = jnp.mean(x * x, axis=-1, keepdims=True)
    return (x * lax.rsqrt(ms + RMS_EPS) * g) * (1.0 + sc) + sh


def _inproj_kernel(x_ref, g_ref, sc_ref, sh_ref, w_ref, o_ref, h_ref):
    @pl.when(pl.program_id(2) == 0)
    def _():
        h_ref[...] = _modulated_norm(x_ref[...], g_ref[...], sc_ref[...], sh_ref[...]).astype(BF16)

    o_ref[...] = _dot(h_ref[...], w_ref[...])


def _inproj(x, g, mod4, w, tm, tn):
    nb, t, d = x.shape
    n = w.shape[1]
    return pl.pallas_call(
        _inproj_kernel,
        grid=(nb, t // tm, n // tn),
        in_specs=[pl.BlockSpec((None, tm, d), lambda b, i, j: (b, i, 0)),
                  pl.BlockSpec((1, d), lambda b, i, j: (0, 0)),
                  pl.BlockSpec((None, None, 1, d), lambda b, i, j: (b, 1, 0, 0)),
                  pl.BlockSpec((None, None, 1, d), lambda b, i, j: (b, 0, 0, 0)),
                  pl.BlockSpec((d, tn), lambda b, i, j: (0, j))],
        out_specs=pl.BlockSpec((None, tm, tn), lambda b, i, j: (b, i, j)),
        out_shape=jax.ShapeDtypeStruct((nb, t, n), F32),
        scratch_shapes=[pltpu.VMEM((tm, d), BF16)],
        compiler_params=_cparams(("arbitrary", "arbitrary", "arbitrary")),
        name="inproj",
    )(x, g.reshape(1, d), mod4, mod4, w)


def _conv_kernel(val_ref, gate_ref, hval_ref, hgate_ref, cw_ref, cb_ref, gg_ref, gb_ref,
                 o_ref, uext_ref, acc_ref):
    tm, cw = val_ref.shape
    first = pl.program_id(1) == 0
    uh = hval_ref[...] * _sigmoid(hgate_ref[...])
    uext_ref[0:HALO, :] = jnp.where(first, 0.0, uh)
    uext_ref[HALO:, :] = val_ref[...] * _sigmoid(gate_ref[...])

    rb = 64
    base = HALO - (CONV_KERNEL - 1)
    for c in range(cw // LANES):
        sl = slice(c * LANES, (c + 1) * LANES)
        for r in range(tm // rb):
            acc = jnp.zeros((rb, LANES), F32) + cb_ref[:, sl]
            for j in range(CONV_KERNEL):
                lo = r * rb + base + j
                acc = acc + cw_ref[j:j + 1, sl] * uext_ref[lo:lo + rb, sl]
            acc_ref[r * rb:(r + 1) * rb, sl] = acc

    ones_bd = _block_diag_ones()
    u = acc_ref[...]
    mean = _head_sum(u, ones_bd) * (1.0 / HEAD_DIM)
    d = u - mean
    var = _head_sum(d * d, ones_bd) * (1.0 / HEAD_DIM)
    y = d * lax.rsqrt(var + CONV_GN_EPS) * gg_ref[...] + gb_ref[...]
    o_ref[...] = (y * _sigmoid(y)).astype(o_ref.dtype)


def _conv_branch(p, conv_w, conv_b, gn_g, gn_b, cw, tm):
    nb, t, _ = p.shape
    hb = tm // HALO
    row = lambda v: v.reshape(1, cw)
    halo_map = lambda col: (lambda b, i: (b, jnp.maximum(i * hb - 1, 0), col))
    return pl.pallas_call(
        _conv_kernel,
        grid=(nb, t // tm),
        in_specs=[pl.BlockSpec((None, tm, cw), lambda b, i: (b, i, 0)),
                  pl.BlockSpec((None, tm, cw), lambda b, i: (b, i, 1)),
                  pl.BlockSpec((None, HALO, cw), halo_map(0)),
                  pl.BlockSpec((None, HALO, cw), halo_map(1)),
                  pl.BlockSpec((CONV_KERNEL, cw), lambda b, i: (0, 0)),
                  pl.BlockSpec((1, cw), lambda b, i: (0, 0)),
                  pl.BlockSpec((1, cw), lambda b, i: (0, 0)),
                  pl.BlockSpec((1, cw), lambda b, i: (0, 0))],
        out_specs=pl.BlockSpec((None, tm, cw), lambda b, i: (b, i, 0)),
        out_shape=jax.ShapeDtypeStruct((nb, t, cw), BF16),
        scratch_shapes=[pltpu.VMEM((tm + HALO, cw), F32), pltpu.VMEM((tm, cw), F32)],
        compiler_params=_cparams(("arbitrary", "arbitrary")),
        name="conv",
    )(p, p, p, p, conv_w, row(conv_b), row(gn_g), row(gn_b))


def _prep_kernel(pr_ref, pk_ref, pv_ref, pl_ref, hr_ref, hk_ref, hv_ref, hl_ref,
                 mu_rkv_ref, mu_lora_ref,
                 w0_ref, ww2_ref, a0_ref, aw2_ref, gw2_ref, kk_ref, ka_ref, rk_ref,
                 r_ref, k_ref, v_ref, an_ref, b_ref, lw_ref, g_ref, bonus_ref):
    tm, rw = r_ref.shape
    first = pl.program_id(1) == 0
    rows = lax.broadcasted_iota(jnp.int32, (tm, 1), 0)

    def shifted(cur_ref, halo_ref, mu):
        cur = cur_ref[...]
        last = jnp.where(first, 0.0, halo_ref[7:8, :])
        prev = jnp.where(rows == 0, last, pltpu.roll(cur, 1, axis=0))
        return cur + (prev - cur) * mu

    r = shifted(pr_ref, hr_ref, mu_rkv_ref[:, :rw])
    k = shifted(pk_ref, hk_ref, mu_rkv_ref[:, rw:2 * rw])
    v = shifted(pv_ref, hv_ref, mu_rkv_ref[:, 2 * rw:])
    ql = shifted(pl_ref, hl_ref, mu_lora_ref[...])
    xw, xa, xg = ql[:, :LANES], ql[:, LANES:2 * LANES], ql[:, 2 * LANES:]

    z = w0_ref[...] + _dot(jnp.tanh(xw).astype(BF16), ww2_ref[...])
    lw_ref[...] = (-math.exp(-0.5)) * _sigmoid(z)
    a = _sigmoid(a0_ref[...] + _dot(xa.astype(BF16), aw2_ref[...]))
    g_ref[...] = _dot(_sigmoid(xg).astype(BF16), gw2_ref[...])

    ones_bd = _block_diag_ones()
    kk = k * kk_ref[...]
    ss = _head_sum(kk * kk, ones_bd)
    kkn = kk * lax.rsqrt(jnp.maximum(ss, 1e-24))
    kmod = k * (1.0 + (a - 1.0) * ka_ref[...])
    r_ref[...] = r
    k_ref[...] = kmod
    v_ref[...] = v
    an_ref[...] = -kkn
    b_ref[...] = kkn * a
    bonus_ref[...] = _head_sum(r * kmod * rk_ref[...], ones_bd) * v


def _prep(p, mu_rkv, mu_lora, w0, ww2, a0, aw2, gw2, k_k, k_a, r_k, rw, conv_cols, tm):
    nb, t, _ = p.shape
    nrkv = 3 * rw
    nl = 4 * LANES
    assert conv_cols % rw == 0 and (conv_cols + nrkv) % nl == 0
    r_blk = conv_cols // rw
    lora_blk = (conv_cols + nrkv) // nl
    hb = tm // 8
    row = lambda v_: v_.reshape(1, -1)
    full = lambda shape: pl.BlockSpec(shape, lambda b, i: (0,) * len(shape))
    cur = lambda w, blk: pl.BlockSpec((None, tm, w), lambda b, i: (b, i, blk))
    halo = lambda w, blk: pl.BlockSpec((None, 8, w), lambda b, i: (b, jnp.maximum(i * hb - 1, 0), blk))
    out_spec = pl.BlockSpec((None, tm, rw), lambda b, i: (b, i, 0))
    return pl.pallas_call(
        _prep_kernel,
        grid=(nb, t // tm),
        in_specs=[cur(rw, r_blk), cur(rw, r_blk + 1), cur(rw, r_blk + 2), cur(nl, lora_blk),
                  halo(rw, r_blk), halo(rw, r_blk + 1), halo(rw, r_blk + 2), halo(nl, lora_blk),
                  full((1, nrkv)), full((1, nl)),
                  full((1, rw)), full((LANES, rw)), full((1, rw)), full((LANES, rw)),
                  full((2 * LANES, rw)), full((1, rw)), full((1, rw)), full((1, rw))],
        out_specs=[out_spec] * 8,
        out_shape=[jax.ShapeDtypeStruct((nb, t, rw), F32)] * 8,
        compiler_params=_cparams(("arbitrary", "arbitrary")),
        name="prep",
    )(p, p, p, p, p, p, p, p, row(mu_rkv), row(mu_lora), row(w0), ww2, row(a0), aw2, gw2,
      row(k_k), row(k_a), row(r_k))


def _wkv_kernel(r_ref, k_ref, v_ref, an_ref, b_ref, lw_ref, y_ref, s_ref):
    tb, rw = r_ref.shape
    ngroups = rw // GROUP
    reps = GROUP // CHUNK

    @pl.when(pl.program_id(1) == 0)
    def _():
        s_ref[...] = jnp.zeros_like(s_ref)

    row = lax.broadcasted_iota(jnp.int32, (GROUP, GROUP), 0)
    col = lax.broadcasted_iota(jnp.int32, (GROUP, GROUP), 1)
    bd = (row // HEAD_DIM) == (col // HEAD_DIM)
    tri_strict = bd & (row > col)
    tri_incl = bd & (row >= col)
    eye = row == col
    tr = lax.broadcasted_iota(jnp.int32, (CHUNK, CHUNK), 0)
    tc = lax.broadcasted_iota(jnp.int32, (CHUNK, CHUNK), 1)
    ltri = jnp.where(tr >= tc, 1.0, 0.0).astype(BF16)

    def tile(x):
        return jnp.concatenate([x] * reps, axis=0)

    def untile(x):
        x = jnp.where(bd, x, 0.0)
        out = x[0:CHUNK]
        for h in range(1, reps):
            out = out + x[h * CHUNK:(h + 1) * CHUNK]
        return out

    def chunk_body(c, carry):
        rs = pl.ds(pl.multiple_of(c * CHUNK, CHUNK), CHUNK)
        for g in range(ngroups):
            ls = slice(g * GROUP, (g + 1) * GROUP)
            lw = lw_ref[rs, ls]
            hi, mid, lo = _split3(lw)
            cum = _dot(ltri, hi) + _dot(ltri, mid) + _dot(ltri, lo)
            e_pos = jnp.exp(cum)
            e_neg = jnp.exp(-cum)
            e_prev = jnp.exp(cum - lw)
            tail = cum[CHUNK - 1:CHUNK, :]
            e_tail = jnp.exp(tail - cum)
            p_end = jnp.exp(tail)
            kg, bg, vg = k_ref[rs, ls], b_ref[rs, ls], v_ref[rs, ls]
            a_t = jnp.where(bd, tile(an_ref[rs, ls] * e_prev), 0.0).astype(BF16)
            r_t = jnp.where(bd, tile(r_ref[rs, ls] * e_pos), 0.0).astype(BF16)
            k_t = tile(kg * e_neg).astype(BF16)
            b_t = tile(bg * e_neg).astype(BF16)
            v_t = tile(vg).astype(BF16)
            ar = jnp.concatenate([a_t, r_t], axis=0)
            kb = jnp.concatenate([k_t, b_t], axis=0)
            amat = _dot_nt(ar, kb)
            a_ak = jnp.where(tri_strict, amat[:GROUP, :GROUP], 0.0).astype(BF16)
            x = jnp.where(tri_strict, amat[:GROUP, GROUP:], 0.0)
            a_rk = jnp.where(tri_incl, amat[GROUP:, :GROUP], 0.0).astype(BF16)
            a_rb = jnp.where(tri_incl, amat[GROUP:, GROUP:], 0.0).astype(BF16)

            t_inv = jnp.where(eye, 1.0, 0.0) + x
            xb = x.astype(BF16)
            xp = _dot(xb, xb)
            nlev = int(math.log2(CHUNK)) - 1
            for lev in range(nlev):
                xpb = xp.astype(BF16)
                if lev < nlev - 1:
                    both = _dot(jnp.concatenate([xpb, t_inv.astype(BF16)], axis=0), xpb)
                    xp = both[:GROUP]
                    t_inv = t_inv + both[GROUP:]
                else:
                    t_inv = t_inv + _dot(t_inv.astype(BF16), xpb)

            s = s_ref[g]
            sb = s.astype(BF16)
            ars = _dot_nt(ar, sb)
            z = ars[:GROUP] + _dot(a_ak, v_t)
            u = _dot(t_inv.astype(BF16), z.astype(BF16))
            ub = u.astype(BF16)
            y = ars[GROUP:] + _dot(jnp.concatenate([a_rk, a_rb], axis=1),
                                   jnp.concatenate([v_t, ub], axis=0))
            y_ref[rs, ls] = untile(y)
            uc = untile(u)
            vu = jnp.concatenate([vg, uc], axis=0).astype(BF16)
            kbe = jnp.concatenate([kg * e_tail, bg * e_tail], axis=0).astype(BF16)
            ds = _dot_tn(vu, kbe)
            s_ref[g] = s * p_end + jnp.where(bd, ds, 0.0)
        return carry

    lax.fori_loop(0, tb // CHUNK, chunk_body, 0)


def _wkv(r, k, v, an, b, lw, tb):
    nb, t, rw = r.shape
    spec = pl.BlockSpec((None, tb, rw), lambda bi, i: (bi, i, 0))
    return pl.pallas_call(
        _wkv_kernel,
        grid=(nb, t // tb),
        in_specs=[spec] * 6,
        out_specs=spec,
        out_shape=jax.ShapeDtypeStruct((nb, t, rw), F32),
        scratch_shapes=[pltpu.VMEM((rw // GROUP, GROUP, GROUP), F32)],
        compiler_params=_cparams(("arbitrary", "arbitrary")),
        name="wkv",
    )(r, k, v, an, b, lw)


def _post_kernel(y_ref, bonus_ref, g_ref, lg_ref, lb_ref, o_ref):
    ones_bd = _block_diag_ones()
    y = y_ref[...]
    mean = _head_sum(y, ones_bd) * (1.0 / HEAD_DIM)
    d = y - mean
    var = _head_sum(d * d, ones_bd) * (1.0 / HEAD_DIM)
    yn = d * lax.rsqrt(var + RWKV_GN_EPS) * lg_ref[...] + lb_ref[...]
    o_ref[...] = ((yn + bonus_ref[...]) * g_ref[...]).astype(o_ref.dtype)


def _post(y, bonus, g, lnx_g, lnx_b, tm):
    nb, t, rw = y.shape
    spec = pl.BlockSpec((None, tm, rw), lambda b, i: (b, i, 0))
    vec = pl.BlockSpec((1, rw), lambda b, i: (0, 0))
    return pl.pallas_call(
        _post_kernel,
        grid=(nb, t // tm),
        in_specs=[spec, spec, spec, vec, vec],
        out_specs=spec,
        out_shape=jax.ShapeDtypeStruct((nb, t, rw), BF16),
        compiler_params=_cparams(("arbitrary", "arbitrary")),
        name="post",
    )(y, bonus, g, lnx_g.reshape(1, rw), lnx_b.reshape(1, rw))


def _outproj_kernel(oc_ref, orw_ref, wa_ref, wb_ref, x_ref, g1_ref, o_ref):
    acc = _dot(oc_ref[...], wa_ref[...]) + _dot(orw_ref[...], wb_ref[...])
    o_ref[...] = x_ref[...] + g1_ref[...] * acc


def _outproj(oc, orw, w_out2, x, mod4, tm, tn):
    nb, t, d = x.shape
    cw, rw = oc.shape[2], orw.shape[2]
    return pl.pallas_call(
        _outproj_kernel,
        grid=(nb, t // tm, d // tn),
        in_specs=[pl.BlockSpec((None, tm, cw), lambda b, i, j: (b, i, 0)),
                  pl.BlockSpec((None, tm, rw), lambda b, i, j: (b, i, 0)),
                  pl.BlockSpec((None, cw, tn), lambda b, i, j: (0, 0, j)),
                  pl.BlockSpec((None, rw, tn), lambda b, i, j: (1, 0, j)),
                  pl.BlockSpec((None, tm, tn), lambda b, i, j: (b, i, j)),
                  pl.BlockSpec((None, None, 1, tn), lambda b, i, j: (b, 2, 0, j))],
        out_specs=pl.BlockSpec((None, tm, tn), lambda b, i, j: (b, i, j)),
        out_shape=jax.ShapeDtypeStruct((nb, t, d), F32),
        compiler_params=_cparams(("arbitrary", "arbitrary", "arbitrary")),
        name="outproj",
    )(oc, orw, w_out2, w_out2, x, mod4)


def _ffn_kernel(x_ref, g_ref, sc_ref, sh_ref, gate_ref, gf_ref, w1_ref, w3_ref, w2_ref,
                o_ref, h_ref, acc_ref, *, final_norm):
    j = pl.program_id(2)

    @pl.when(j == 0)
    def _():
        h_ref[...] = _modulated_norm(x_ref[...], g_ref[...], sc_ref[...], sh_ref[...]).astype(BF16)
        acc_ref[...] = jnp.zeros_like(acc_ref)

    h = h_ref[...]
    a = _dot(h, w1_ref[...])
    b = _dot(h, w3_ref[...])
    act = (a * _sigmoid(a) * b).astype(BF16)
    acc_ref[...] += _dot(act, w2_ref[...])

    @pl.when(j == pl.num_programs(2) - 1)
    def _():
        xo = x_ref[...] + gate_ref[...] * acc_ref[...]
        if final_norm:
            ms = jnp.mean(xo * xo, axis=-1, keepdims=True)
            xo = xo * lax.rsqrt(ms + RMS_EPS) * gf_ref[...]
        o_ref[...] = xo


def _ffn(x1, g, mod4, gf, w1, w3, w2, tm, tf, final_norm):
    nb, t, d = x1.shape
    f = w1.shape[1]
    modspec = lambda comp: pl.BlockSpec((None, None, 1, d), lambda b, i, j: (b, comp, 0, 0))
    vec = pl.BlockSpec((1, d), lambda b, i, j: (0, 0))
    return pl.pallas_call(
        functools.partial(_ffn_kernel, final_norm=final_norm),
        grid=(nb, t // tm, f // tf),
        in_specs=[pl.BlockSpec((None, tm, d), lambda b, i, j: (b, i, 0)),
                  vec, modspec(4), modspec(3), modspec(5), vec,
                  pl.BlockSpec((d, tf), lambda b, i, j: (0, j)),
                  pl.BlockSpec((d, tf), lambda b, i, j: (0, j)),
                  pl.BlockSpec((tf, d), lambda b, i, j: (j, 0))],
        out_specs=pl.BlockSpec((None, tm, d), lambda b, i, j: (b, i, 0)),
        out_shape=jax.ShapeDtypeStruct((nb, t, d), F32),
        scratch_shapes=[pltpu.VMEM((tm, d), BF16), pltpu.VMEM((tm, d), F32)],
        compiler_params=_cparams(("arbitrary", "arbitrary", "arbitrary")),
        name="ffn",
    )(x1, g.reshape(1, d), mod4, mod4, mod4, gf.reshape(1, d), w1, w3, w2)


def _pad_rows(w, n):
    return jnp.concatenate([w, jnp.zeros((n - w.shape[0],) + w.shape[1:], w.dtype)], axis=0)


def _pad_cols(w, n):
    return jnp.concatenate([w, jnp.zeros(w.shape[:-1] + (n - w.shape[-1],), w.dtype)], axis=-1)


def _tile(n, pref):
    return pref if n % pref == 0 else n


def kernel(x, c, w_ada, b_ada, norm1_g, w_in, tshift_mu, conv_w, conv_b, conv_gn_g, conv_gn_b,
           w0, w_w2, a0, a_w2, g_w2, k_k, k_a, r_k, lnx_g, lnx_b, w_out, norm2_g,
           w_ff1, w_ff3, w_ff2, norm_f_g):
    nb, t, d = x.shape
    depth = w_in.shape[0]
    cw = conv_w.shape[2]
    rw = w0.shape[1]
    dl, al, gl = w_w2.shape[1], a_w2.shape[1], g_w2.shape[1]
    conv_cols = 2 * cw
    assert dl <= LANES and al <= LANES and gl == 2 * LANES and rw % GROUP == 0

    tm_mm = _tile(t, 512)
    for l in range(depth):
        mod = _ada(c, w_ada[l], b_ada[l])
        mod4 = mod.reshape(nb, N_MOD, 1, d)

        o3 = conv_cols + 3 * rw
        wi = w_in[l]
        wi = jnp.concatenate([wi[:, :o3], _pad_cols(wi[:, o3:o3 + dl], LANES),
                              _pad_cols(wi[:, o3 + dl:o3 + dl + al], LANES),
                              wi[:, o3 + dl + al:]], axis=1).astype(BF16)
        mu = tshift_mu[l]
        mu_rkv = mu[:3 * rw]
        mu_lora = jnp.concatenate([_pad_cols(mu[3 * rw:3 * rw + dl], LANES),
                                   _pad_cols(mu[3 * rw + dl:3 * rw + dl + al], LANES),
                                   mu[3 * rw + dl + al:]])
        ww2 = _pad_rows(w_w2[l], LANES).astype(BF16)
        aw2 = _pad_rows(a_w2[l], LANES).astype(BF16)
        gw2 = g_w2[l].astype(BF16)

        p = _inproj(x, norm1_g[l], mod4, wi, tm_mm, 512)
        oc = _conv_branch(p, conv_w[l], conv_b[l], conv_gn_g[l], conv_gn_b[l], cw, _tile(t, 256))
        r, k, v, an, b, lw, g, bonus = _prep(p, mu_rkv, mu_lora, w0[l], ww2, a0[l], aw2, gw2,
                                             k_k[l], k_a[l], r_k[l].reshape(-1), rw, conv_cols,
                                             _tile(t, 256))
        y = _wkv(r, k, v, an, b, lw, _tile(t, 256))
        orw = _post(y, bonus, g, lnx_g[l], lnx_b[l], _tile(t, 512))
        x = _outproj(oc, orw, w_out[l].astype(BF16).reshape(2, cw, d), x, mod4, tm_mm, 512)
        x = _ffn(x, norm2_g[l], mod4, norm_f_g, w_ff1[l].astype(BF16), w_ff3[l].astype(BF16),
                 w_ff2[l].astype(BF16), tm_mm, 512, final_norm=(l == depth - 1))
    return x
```

```python
import functools
import math

import jax
import jax.numpy as jnp
from jax import lax
from jax.experimental import pallas as pl
from jax.experimental.pallas import tpu as pltpu

F32 = jnp.float32
BF16 = jnp.bfloat16

HEAD_DIM = 64
CONV_KERNEL = 31
RMS_EPS = 1e-6
CONV_GN_EPS = 1e-5
RWKV_GN_EPS = 64e-5
N_MOD = 6

LANES = 128
SUBLANES = 8
BF16_ROWS = 16
MXU_DIM = 256
GROUP = MXU_DIM
CHUNK = 64
HALO = 32
VMEM_LIMIT = 52 * 1024 * 1024


def _cparams(sem):
    return pltpu.CompilerParams(dimension_semantics=sem, vmem_limit_bytes=VMEM_LIMIT)


def _dot(a, b):
    return jnp.dot(a, b, preferred_element_type=F32)


def _dot_nt(a, b):
    return lax.dot_general(a, b, (((1,), (1,)), ((), ())), preferred_element_type=F32)


def _dot_tn(a, b):
    return lax.dot_general(a, b, (((0,), (0,)), ((), ())), preferred_element_type=F32)


def _split2(x):
    hi = x.astype(BF16)
    lo = (x - hi.astype(F32)).astype(BF16)
    return hi, lo


def _split3(x):
    hi = x.astype(BF16)
    r1 = x - hi.astype(F32)
    mid = r1.astype(BF16)
    lo = (r1 - mid.astype(F32)).astype(BF16)
    return hi, mid, lo


def _sigmoid(x):
    return 1.0 / (1.0 + jnp.exp(-x))


def _head_sum(x, ones_bd):
    outs = []
    for g in range(x.shape[1] // GROUP):
        xs = x[:, g * GROUP:(g + 1) * GROUP]
        hi, lo = _split2(xs)
        outs.append(_dot(hi, ones_bd) + _dot(lo, ones_bd))
    return outs[0] if len(outs) == 1 else jnp.concatenate(outs, axis=1)


def _block_diag_ones():
    r = lax.broadcasted_iota(jnp.int32, (GROUP, GROUP), 0) // HEAD_DIM
    c = lax.broadcasted_iota(jnp.int32, (GROUP, GROUP), 1) // HEAD_DIM
    return jnp.where(r == c, 1.0, 0.0).astype(BF16)


def _ada_kernel(cb_ref, w_ref, b_ref, o_ref):
    nb = cb_ref.shape[0]
    tn = w_ref.shape[1]
    for b in range(nb):
        cb = cb_ref[b]
        sc = cb * _sigmoid(cb)
        for j in range(tn // LANES):
            sl = slice(j * LANES, (j + 1) * LANES)
            s = jnp.sum(w_ref[:, sl] * sc, axis=0, keepdims=True)
            o_ref[b:b + 1, sl] = s + b_ref[:, sl]


def _ada(c, w_ada, b_ada):
    nb, d = c.shape
    n = w_ada.shape[1]
    tn = 1024
    cb = jnp.broadcast_to(c[:, :, None], (nb, d, LANES))
    return pl.pallas_call(
        _ada_kernel,
        grid=(n // tn,),
        in_specs=[pl.BlockSpec((nb, d, LANES), lambda j: (0, 0, 0)),
                  pl.BlockSpec((d, tn), lambda j: (0, j)),
                  pl.BlockSpec((1, tn), lambda j: (0, j))],
        out_specs=pl.BlockSpec((nb, tn), lambda j: (0, j)),
        out_shape=jax.ShapeDtypeStruct((nb, n), F32),
        compiler_params=_cparams(("arbitrary",)),
        name="ada",
    )(cb, w_ada, b_ada.reshape(1, n))


def _modulated_norm(x, g, sc, sh):
    ms = jnp.mean(x * x, axis=-1, keepdims=True)
    return (x * lax.rsqrt(ms + RMS_EPS) * g) * (1.0 + sc) + sh


def _inproj_kernel(x_ref, g_ref, sc_ref, sh_ref, w_ref, o_ref, *, tn):
    h = _modulated_norm(x_ref[...], g_ref[...], sc_ref[...], sh_ref[...]).astype(BF16)
    for j in range(w_ref.shape[1] // tn):
        sl = slice(j * tn, (j + 1) * tn)
        o_ref[:, sl] = _dot(h, w_ref[:, sl]).astype(o_ref.dtype)


def _inproj(x, g, mod4, w, tm, tn):
    nb, t, d = x.shape
    n = w.shape[1]
    return pl.pallas_call(
        functools.partial(_inproj_kernel, tn=tn),
        grid=(nb, t // tm),
        in_specs=[pl.BlockSpec((None, tm, d), lambda b, i: (b, i, 0)),
                  pl.BlockSpec((1, d), lambda b, i: (0, 0)),
                  pl.BlockSpec((None, None, 1, d), lambda b, i: (b, 1, 0, 0)),
                  pl.BlockSpec((None, None, 1, d), lambda b, i: (b, 0, 0, 0)),
                  pl.BlockSpec((d, n), lambda b, i: (0, 0), pipeline_mode=pl.Buffered(1))],
        out_specs=pl.BlockSpec((None, tm, n), lambda b, i: (b, i, 0)),
        out_shape=jax.ShapeDtypeStruct((nb, t, n), BF16),
        compiler_params=_cparams(("arbitrary", "arbitrary")),
        name="inproj",
    )(x, g.reshape(1, d), mod4, mod4, w)


def _conv_kernel(val_ref, gate_ref, hval_ref, hgate_ref, cw_ref, cb_ref, gg_ref, gb_ref,
                 o_ref, ush_ref, acc_ref):
    tm, cw = val_ref.shape
    nsh = ush_ref.shape[1]
    first = pl.program_id(1) == 0
    uh = hval_ref[...].astype(F32) * _sigmoid(hgate_ref[...].astype(F32))
    ush_ref[0, 0:HALO, :] = jnp.where(first, 0.0, uh)
    ush_ref[0, HALO:HALO + tm, :] = val_ref[...].astype(F32) * _sigmoid(gate_ref[...].astype(F32))
    for phi in range(1, SUBLANES):
        ush_ref[phi, 0:nsh - SUBLANES, :] = ush_ref[0, phi:phi + nsh - SUBLANES, :]

    rb = 64
    base = HALO - (CONV_KERNEL - 1)
    for c in range(cw // LANES):
        sl = slice(c * LANES, (c + 1) * LANES)
        for r in range(tm // rb):
            acc = jnp.zeros((rb, LANES), F32) + cb_ref[:, sl]
            for j in range(CONV_KERNEL):
                q, phi = divmod(base + j, SUBLANES)
                lo = r * rb + q * SUBLANES
                acc = acc + cw_ref[j:j + 1, sl] * ush_ref[phi, lo:lo + rb, sl]
            acc_ref[r * rb:(r + 1) * rb, sl] = acc

    ones_bd = _block_diag_ones()
    u = acc_ref[...]
    mean = _head_sum(u, ones_bd) * (1.0 / HEAD_DIM)
    d = u - mean
    var = _head_sum(d * d, ones_bd) * (1.0 / HEAD_DIM)
    y = d * lax.rsqrt(var + CONV_GN_EPS) * gg_ref[...] + gb_ref[...]
    o_ref[...] = (y * _sigmoid(y)).astype(o_ref.dtype)


def _conv_branch(p, conv_w, conv_b, gn_g, gn_b, cw, tm):
    nb, t, _ = p.shape
    hb = tm // HALO
    row = lambda v: v.reshape(1, cw)
    halo_map = lambda col: (lambda b, i: (b, jnp.maximum(i * hb - 1, 0), col))
    return pl.pallas_call(
        _conv_kernel,
        grid=(nb, t // tm),
        in_specs=[pl.BlockSpec((None, tm, cw), lambda b, i: (b, i, 0)),
                  pl.BlockSpec((None, tm, cw), lambda b, i: (b, i, 1)),
                  pl.BlockSpec((None, HALO, cw), halo_map(0)),
                  pl.BlockSpec((None, HALO, cw), halo_map(1)),
                  pl.BlockSpec((CONV_KERNEL, cw), lambda b, i: (0, 0)),
                  pl.BlockSpec((1, cw), lambda b, i: (0, 0)),
                  pl.BlockSpec((1, cw), lambda b, i: (0, 0)),
                  pl.BlockSpec((1, cw), lambda b, i: (0, 0))],
        out_specs=pl.BlockSpec((None, tm, cw), lambda b, i: (b, i, 0)),
        out_shape=jax.ShapeDtypeStruct((nb, t, cw), BF16),
        scratch_shapes=[pltpu.VMEM((SUBLANES, tm + HALO, cw), F32), pltpu.VMEM((tm, cw), F32)],
        compiler_params=_cparams(("arbitrary", "arbitrary")),
        name="conv",
    )(p, p, p, p, conv_w, row(conv_b), row(gn_g), row(gn_b))


def _prep_kernel(pr_ref, pk_ref, pv_ref, pl_ref, hr_ref, hk_ref, hv_ref, hl_ref,
                 mu_rkv_ref, mu_lora_ref,
                 w0_ref, ww2_ref, a0_ref, aw2_ref, gw2_ref, kk_ref, ka_ref, rk_ref,
                 r_ref, k_ref, v_ref, an_ref, b_ref, lw_ref, g_ref, bonus_ref):
    tm, rw = r_ref.shape
    first = pl.program_id(1) == 0
    rows = lax.broadcasted_iota(jnp.int32, (tm, 1), 0)

    def shifted(cur_ref, halo_ref, mu):
        cur = cur_ref[...].astype(F32)
        last = jnp.where(first, 0.0, halo_ref[BF16_ROWS - 1:BF16_ROWS, :].astype(F32))
        prev = jnp.where(rows == 0, last, pltpu.roll(cur, 1, axis=0))
        return cur + (prev - cur) * mu

    r = shifted(pr_ref, hr_ref, mu_rkv_ref[:, :rw])
    k = shifted(pk_ref, hk_ref, mu_rkv_ref[:, rw:2 * rw])
    v = shifted(pv_ref, hv_ref, mu_rkv_ref[:, 2 * rw:])
    ql = shifted(pl_ref, hl_ref, mu_lora_ref[...])
    xw, xa, xg = ql[:, :LANES], ql[:, LANES:2 * LANES], ql[:, 2 * LANES:]

    z = w0_ref[...] + _dot(jnp.tanh(xw).astype(BF16), ww2_ref[...])
    lw_ref[...] = (-math.exp(-0.5)) * _sigmoid(z)
    a = _sigmoid(a0_ref[...] + _dot(xa.astype(BF16), aw2_ref[...]))
    g_ref[...] = _dot(_sigmoid(xg).astype(BF16), gw2_ref[...]).astype(g_ref.dtype)

    ones_bd = _block_diag_ones()
    kk = k * kk_ref[...]
    ss = _head_sum(kk * kk, ones_bd)
    kkn = kk * lax.rsqrt(jnp.maximum(ss, 1e-24))
    kmod = k * (1.0 + (a - 1.0) * ka_ref[...])
    r_ref[...] = r.astype(r_ref.dtype)
    k_ref[...] = kmod.astype(k_ref.dtype)
    v_ref[...] = v.astype(v_ref.dtype)
    an_ref[...] = (-kkn).astype(an_ref.dtype)
    b_ref[...] = (kkn * a).astype(b_ref.dtype)
    bonus_ref[...] = (_head_sum(r * kmod * rk_ref[...], ones_bd) * v).astype(bonus_ref.dtype)


def _prep(p, mu_rkv, mu_lora, w0, ww2, a0, aw2, gw2, k_k, k_a, r_k, rw, conv_cols, tm):
    nb, t, _ = p.shape
    nrkv = 3 * rw
    nl = 4 * LANES
    assert conv_cols % rw == 0 and (conv_cols + nrkv) % nl == 0
    r_blk = conv_cols // rw
    lora_blk = (conv_cols + nrkv) // nl
    hb = tm // BF16_ROWS
    row = lambda v_: v_.reshape(1, -1)
    full = lambda shape: pl.BlockSpec(shape, lambda b, i: (0,) * len(shape))
    cur = lambda w, blk: pl.BlockSpec((None, tm, w), lambda b, i: (b, i, blk))
    halo = lambda w, blk: pl.BlockSpec((None, BF16_ROWS, w),
                                       lambda b, i: (b, jnp.maximum(i * hb - 1, 0), blk))
    out_spec = pl.BlockSpec((None, tm, rw), lambda b, i: (b, i, 0))
    bf = jax.ShapeDtypeStruct((nb, t, rw), BF16)
    return pl.pallas_call(
        _prep_kernel,
        grid=(nb, t // tm),
        in_specs=[cur(rw, r_blk), cur(rw, r_blk + 1), cur(rw, r_blk + 2), cur(nl, lora_blk),
                  halo(rw, r_blk), halo(rw, r_blk + 1), halo(rw, r_blk + 2), halo(nl, lora_blk),
                  full((1, nrkv)), full((1, nl)),
                  full((1, rw)), full((LANES, rw)), full((1, rw)), full((LANES, rw)),
                  full((2 * LANES, rw)), full((1, rw)), full((1, rw)), full((1, rw))],
        out_specs=[out_spec] * 8,
        out_shape=[bf, bf, bf, bf, bf, jax.ShapeDtypeStruct((nb, t, rw), F32), bf, bf],
        compiler_params=_cparams(("arbitrary", "arbitrary")),
        name="prep",
    )(p, p, p, p, p, p, p, p, row(mu_rkv), row(mu_lora), row(w0), ww2, row(a0), aw2, gw2,
      row(k_k), row(k_a), row(r_k))


def _wkv_kernel(r_ref, k_ref, v_ref, an_ref, b_ref, lw_ref, y_ref, s_ref):
    nb, tb, rw = r_ref.shape
    ngroups = rw // GROUP
    reps = GROUP // CHUNK
    streams = [(bi, g) for bi in range(nb) for g in range(ngroups)]

    @pl.when(pl.program_id(0) == 0)
    def _():
        s_ref[...] = jnp.zeros_like(s_ref)

    row = lax.broadcasted_iota(jnp.int32, (GROUP, GROUP), 0)
    col = lax.broadcasted_iota(jnp.int32, (GROUP, GROUP), 1)
    bd = (row // HEAD_DIM) == (col // HEAD_DIM)
    bd16 = (row // HEAD_DIM).astype(F32).astype(BF16) == (col // HEAD_DIM).astype(F32).astype(BF16)
    t_idx = lax.broadcasted_iota(jnp.int32, (CHUNK, GROUP), 0)
    j_idx = lax.broadcasted_iota(jnp.int32, (CHUNK, GROUP), 1) % CHUNK
    tj_strict = t_idx > j_idx
    tj_incl = t_idx >= j_idx
    ident = jnp.where(t_idx == j_idx, 1.0, 0.0)
    tr = lax.broadcasted_iota(jnp.int32, (CHUNK, CHUNK), 0)
    tc = lax.broadcasted_iota(jnp.int32, (CHUNK, CHUNK), 1)
    ltri = jnp.where(tr >= tc, 1.0, 0.0).astype(BF16)
    nlev = int(math.log2(CHUNK)) - 1

    def expand(xc):
        xb = xc.astype(BF16)
        return jnp.where(bd16, jnp.concatenate([xb] * reps, axis=0), jnp.zeros((), BF16))

    def stack(a, b):
        return jnp.concatenate([a, b], axis=0).astype(BF16)

    def chunk_body(c, carry):
        rs = pl.ds(pl.multiple_of(c * CHUNK, CHUNK), CHUNK)
        ns = len(streams)
        vg, ar, kb, vexp, kbe, p_end = [], [], [], [], [], []
        for bi, g in streams:
            ls = slice(g * GROUP, (g + 1) * GROUP)
            lw = lw_ref[bi, rs, ls]
            hi, mid, lo = _split3(lw)
            cum = _dot(ltri, hi) + _dot(ltri, mid) + _dot(ltri, lo)
            tail = cum[CHUNK - 1:CHUNK, :]
            e_neg = jnp.exp(-cum)
            e_tail = jnp.exp(tail - cum)
            k_ = k_ref[bi, rs, ls].astype(F32)
            b_ = b_ref[bi, rs, ls].astype(F32)
            v_ = v_ref[bi, rs, ls].astype(F32)
            vg.append(v_)
            ar.append(stack(an_ref[bi, rs, ls].astype(F32) * jnp.exp(cum - lw),
                            r_ref[bi, rs, ls].astype(F32) * jnp.exp(cum)))
            kb.append(jnp.concatenate([expand(k_ * e_neg), expand(b_ * e_neg)], axis=0))
            vexp.append(expand(v_))
            kbe.append(stack(k_ * e_tail, b_ * e_tail))
            p_end.append(jnp.exp(tail))

        amat = [_dot_nt(ar[i], kb[i]) for i in range(ns)]
        a_kc = [stack(jnp.where(tj_strict, amat[i][:CHUNK, :GROUP], 0.0),
                      jnp.where(tj_incl, amat[i][CHUNK:, :GROUP], 0.0)) for i in range(ns)]
        a_rb = [jnp.where(tj_incl, amat[i][CHUNK:, GROUP:], 0.0).astype(BF16) for i in range(ns)]
        xp = [jnp.where(tj_strict, amat[i][:CHUNK, GROUP:], 0.0) for i in range(ns)]

        t_inv = [ident + xp[i] for i in range(ns)]
        xp = [_dot(xp[i].astype(BF16), expand(xp[i])) for i in range(ns)]
        for lev in range(nlev):
            if lev < nlev - 1:
                both = [_dot(stack(xp[i], t_inv[i]), expand(xp[i])) for i in range(ns)]
                xp = [both[i][:CHUNK] for i in range(ns)]
                t_inv = [t_inv[i] + both[i][CHUNK:] for i in range(ns)]
            else:
                t_inv = [t_inv[i] + _dot(t_inv[i].astype(BF16), expand(xp[i])) for i in range(ns)]

        s_old = [s_ref[i] for i in range(ns)]
        ars = [_dot_nt(ar[i], s_old[i].astype(BF16)) for i in range(ns)]
        akv = [_dot(a_kc[i], vexp[i]) for i in range(ns)]
        z = [ars[i][:CHUNK] + akv[i][:CHUNK] for i in range(ns)]
        u = [_dot(t_inv[i].astype(BF16), expand(z[i])) for i in range(ns)]
        y = [ars[i][CHUNK:] + akv[i][CHUNK:] + _dot(a_rb[i], expand(u[i])) for i in range(ns)]
        ds = [_dot_tn(stack(vg[i], u[i]), kbe[i]) for i in range(ns)]
        for i, (bi, g) in enumerate(streams):
            y_ref[bi, rs, g * GROUP:(g + 1) * GROUP] = y[i]
            s_ref[i] = s_old[i] * p_end[i] + jnp.where(bd, ds[i], 0.0)
        return carry

    lax.fori_loop(0, tb // CHUNK, chunk_body, 0)


def _wkv(r, k, v, an, b, lw, tb):
    nb, t, rw = r.shape
    spec = pl.BlockSpec((nb, tb, rw), lambda i: (0, i, 0))
    return pl.pallas_call(
        _wkv_kernel,
        grid=(t // tb,),
        in_specs=[spec] * 6,
        out_specs=spec,
        out_shape=jax.ShapeDtypeStruct((nb, t, rw), F32),
        scratch_shapes=[pltpu.VMEM((nb * (rw // GROUP), GROUP, GROUP), F32)],
        compiler_params=_cparams(("arbitrary",)),
        name="wkv",
    )(r, k, v, an, b, lw)


def _post_kernel(y_ref, bonus_ref, g_ref, lg_ref, lb_ref, o_ref):
    ones_bd = _block_diag_ones()
    y = y_ref[...]
    mean = _head_sum(y, ones_bd) * (1.0 / HEAD_DIM)
    d = y - mean
    var = _head_sum(d * d, ones_bd) * (1.0 / HEAD_DIM)
    yn = d * lax.rsqrt(var + RWKV_GN_EPS) * lg_ref[...] + lb_ref[...]
    o_ref[...] = ((yn + bonus_ref[...].astype(F32)) * g_ref[...].astype(F32)).astype(o_ref.dtype)


def _post(y, bonus, g, lnx_g, lnx_b, tm):
    nb, t, rw = y.shape
    spec = pl.BlockSpec((None, tm, rw), lambda b, i: (b, i, 0))
    vec = pl.BlockSpec((1, rw), lambda b, i: (0, 0))
    return pl.pallas_call(
        _post_kernel,
        grid=(nb, t // tm),
        in_specs=[spec, spec, spec, vec, vec],
        out_specs=spec,
        out_shape=jax.ShapeDtypeStruct((nb, t, rw), BF16),
        compiler_params=_cparams(("arbitrary", "arbitrary")),
        name="post",
    )(y, bonus, g, lnx_g.reshape(1, rw), lnx_b.reshape(1, rw))


def _outproj_kernel(oc_ref, orw_ref, wa_ref, wb_ref, x_ref, g1_ref, o_ref, *, tn):
    oc = oc_ref[...]
    orw = orw_ref[...]
    for j in range(o_ref.shape[1] // tn):
        sl = slice(j * tn, (j + 1) * tn)
        acc = _dot(oc, wa_ref[:, sl]) + _dot(orw, wb_ref[:, sl])
        o_ref[:, sl] = x_ref[:, sl] + g1_ref[:, sl] * acc


def _outproj(oc, orw, w_out2, x, mod4, tm, tn):
    nb, t, d = x.shape
    cw, rw = oc.shape[2], orw.shape[2]
    return pl.pallas_call(
        functools.partial(_outproj_kernel, tn=tn),
        grid=(nb, t // tm),
        in_specs=[pl.BlockSpec((None, tm, cw), lambda b, i: (b, i, 0)),
                  pl.BlockSpec((None, tm, rw), lambda b, i: (b, i, 0)),
                  pl.BlockSpec((None, cw, d), lambda b, i: (0, 0, 0)),
                  pl.BlockSpec((None, rw, d), lambda b, i: (1, 0, 0)),
                  pl.BlockSpec((None, tm, d), lambda b, i: (b, i, 0)),
                  pl.BlockSpec((None, None, 1, d), lambda b, i: (b, 2, 0, 0))],
        out_specs=pl.BlockSpec((None, tm, d), lambda b, i: (b, i, 0)),
        out_shape=jax.ShapeDtypeStruct((nb, t, d), F32),
        compiler_params=_cparams(("arbitrary", "arbitrary")),
        name="outproj",
    )(oc, orw, w_out2, w_out2, x, mod4)


def _ffn_kernel(x_ref, g_ref, sc_ref, sh_ref, gate_ref, gf_ref, w1_ref, w3_ref, w2_ref,
                o_ref, h_ref, acc_ref, *, final_norm):
    j = pl.program_id(2)

    @pl.when(j == 0)
    def _():
        h_ref[...] = _modulated_norm(x_ref[...], g_ref[...], sc_ref[...], sh_ref[...]).astype(BF16)
        acc_ref[...] = jnp.zeros_like(acc_ref)

    h = h_ref[...]
    a = _dot(h, w1_ref[...])
    b = _dot(h, w3_ref[...])
    act = (a * _sigmoid(a) * b).astype(BF16)
    acc_ref[...] += _dot(act, w2_ref[...])

    @pl.when(j == pl.num_programs(2) - 1)
    def _():
        xo = x_ref[...] + gate_ref[...] * acc_ref[...]
        if final_norm:
            ms = jnp.mean(xo * xo, axis=-1, keepdims=True)
            xo = xo * lax.rsqrt(ms + RMS_EPS) * gf_ref[...]
        o_ref[...] = xo


def _ffn(x1, g, mod4, gf, w1, w3, w2, tm, tf, final_norm):
    nb, t, d = x1.shape
    f = w1.shape[1]
    modspec = lambda comp: pl.BlockSpec((None, None, 1, d), lambda b, i, j: (b, comp, 0, 0))
    vec = pl.BlockSpec((1, d), lambda b, i, j: (0, 0))
    return pl.pallas_call(
        functools.partial(_ffn_kernel, final_norm=final_norm),
        grid=(nb, t // tm, f // tf),
        in_specs=[pl.BlockSpec((None, tm, d), lambda b, i, j: (b, i, 0)),
                  vec, modspec(4), modspec(3), modspec(5), vec,
                  pl.BlockSpec((d, tf), lambda b, i, j: (0, j)),
                  pl.BlockSpec((d, tf), lambda b, i, j: (0, j)),
                  pl.BlockSpec((tf, d), lambda b, i, j: (j, 0))],
        out_specs=pl.BlockSpec((None, tm, d), lambda b, i, j: (b, i, 0)),
        out_shape=jax.ShapeDtypeStruct((nb, t, d), F32),
        scratch_shapes=[pltpu.VMEM((tm, d), BF16), pltpu.VMEM((tm, d), F32)],
        compiler_params=_cparams(("arbitrary", "arbitrary", "arbitrary")),
        name="ffn",
    )(x1, g.reshape(1, d), mod4, mod4, mod4, gf.reshape(1, d), w1, w3, w2)


def _pad_rows(w, n):
    return jnp.concatenate([w, jnp.zeros((n - w.shape[0],) + w.shape[1:], w.dtype)], axis=0)


def _pad_cols(w, n):
    return jnp.concatenate([w, jnp.zeros(w.shape[:-1] + (n - w.shape[-1],), w.dtype)], axis=-1)


def _tile(n, pref):
    return pref if n % pref == 0 else n


def kernel(x, c, w_ada, b_ada, norm1_g, w_in, tshift_mu, conv_w, conv_b, conv_gn_g, conv_gn_b,
           w0, w_w2, a0, a_w2, g_w2, k_k, k_a, r_k, lnx_g, lnx_b, w_out, norm2_g,
           w_ff1, w_ff3, w_ff2, norm_f_g):
    nb, t, d = x.shape
    depth = w_in.shape[0]
    cw = conv_w.shape[2]
    rw = w0.shape[1]
    dl, al, gl = w_w2.shape[1], a_w2.shape[1], g_w2.shape[1]
    conv_cols = 2 * cw
    assert dl <= LANES and al <= LANES and gl == 2 * LANES and rw % GROUP == 0

    for l in range(depth):
        mod = _ada(c, w_ada[l], b_ada[l])
        mod4 = mod.reshape(nb, N_MOD, 1, d)

        o3 = conv_cols + 3 * rw
        wi = w_in[l].astype(BF16)
        wi = jnp.concatenate([wi[:, :o3], _pad_cols(wi[:, o3:o3 + dl], LANES),
                              _pad_cols(wi[:, o3 + dl:o3 + dl + al], LANES),
                              wi[:, o3 + dl + al:]], axis=1)
        mu = tshift_mu[l]
        mu_rkv = mu[:3 * rw]
        mu_lora = jnp.concatenate([_pad_cols(mu[3 * rw:3 * rw + dl], LANES),
                                   _pad_cols(mu[3 * rw + dl:3 * rw + dl + al], LANES),
                                   mu[3 * rw + dl + al:]])
        ww2 = _pad_rows(w_w2[l], LANES).astype(BF16)
        aw2 = _pad_rows(a_w2[l], LANES).astype(BF16)
        gw2 = g_w2[l].astype(BF16)

        p = _inproj(x, norm1_g[l], mod4, wi, _tile(t, 256), 512)
        oc = _conv_branch(p, conv_w[l], conv_b[l], conv_gn_g[l], conv_gn_b[l], cw, _tile(t, 256))
        r, k, v, an, b, lw, g, bonus = _prep(p, mu_rkv, mu_lora, w0[l], ww2, a0[l], aw2, gw2,
                                             k_k[l], k_a[l], r_k[l].reshape(-1), rw, conv_cols,
                                             _tile(t, 256))
        y = _wkv(r, k, v, an, b, lw, _tile(t, 256))
        orw = _post(y, bonus, g, lnx_g[l], lnx_b[l], _tile(t, 512))
        x = _outproj(oc, orw, w_out[l].astype(BF16).reshape(2, cw, d), x, mod4, _tile(t, 512), 512)
        x = _ffn(x, norm2_g[l], mod4, norm_f_g, w_ff1[l].astype(BF16), w_ff3[l].astype(BF16),
                 w_ff2[l].astype(BF16), _tile(t, 512), 512, final_norm=(l == depth - 1))
    return x
```

```python
import functools
import math

import jax
import jax.numpy as jnp
from jax import lax
from jax.experimental import pallas as pl
from jax.experimental.pallas import tpu as pltpu

F32 = jnp.float32
BF16 = jnp.bfloat16

HEAD_DIM = 64
CONV_KERNEL = 31
RMS_EPS = 1e-6
CONV_GN_EPS = 1e-5
RWKV_GN_EPS = 64e-5
N_MOD = 6

LANES = 128
SUBLANES = 8
BF16_ROWS = 16
MXU_DIM = 256
GROUP = MXU_DIM
CHUNK = 64
HALO = 32
VMEM_LIMIT = 52 * 1024 * 1024


def _cparams(sem):
    return pltpu.CompilerParams(dimension_semantics=sem, vmem_limit_bytes=VMEM_LIMIT)


def _dot(a, b):
    return jnp.dot(a, b, preferred_element_type=F32)


def _dot_nt(a, b):
    return lax.dot_general(a, b, (((1,), (1,)), ((), ())), preferred_element_type=F32)


def _dot_tn(a, b):
    return lax.dot_general(a, b, (((0,), (0,)), ((), ())), preferred_element_type=F32)


def _split2(x):
    hi = x.astype(BF16)
    lo = (x - hi.astype(F32)).astype(BF16)
    return hi, lo


def _sigmoid(x):
    return 1.0 / (1.0 + jnp.exp(-x))


def _head_sum(x, ones_bd):
    outs = [_dot(x[:, g * GROUP:(g + 1) * GROUP].astype(BF16), ones_bd)
            for g in range(x.shape[1] // GROUP)]
    return outs[0] if len(outs) == 1 else jnp.concatenate(outs, axis=1)


def _block_diag_ones():
    r = lax.broadcasted_iota(jnp.int32, (GROUP, GROUP), 0) // HEAD_DIM
    c = lax.broadcasted_iota(jnp.int32, (GROUP, GROUP), 1) // HEAD_DIM
    return jnp.where(r == c, 1.0, 0.0).astype(BF16)


def _ada_kernel(cb_ref, w_ref, b_ref, o_ref):
    nb = cb_ref.shape[0]
    tn = w_ref.shape[1]
    for b in range(nb):
        cb = cb_ref[b]
        sc = cb * _sigmoid(cb)
        for j in range(tn // LANES):
            sl = slice(j * LANES, (j + 1) * LANES)
            s = jnp.sum(w_ref[:, sl] * sc, axis=0, keepdims=True)
            o_ref[b:b + 1, sl] = s + b_ref[:, sl]


def _ada(c, w_ada, b_ada):
    nb, d = c.shape
    n = w_ada.shape[1]
    tn = 1024
    cb = jnp.broadcast_to(c[:, :, None], (nb, d, LANES))
    return pl.pallas_call(
        _ada_kernel,
        grid=(n // tn,),
        in_specs=[pl.BlockSpec((nb, d, LANES), lambda j: (0, 0, 0)),
                  pl.BlockSpec((d, tn), lambda j: (0, j)),
                  pl.BlockSpec((1, tn), lambda j: (0, j))],
        out_specs=pl.BlockSpec((nb, tn), lambda j: (0, j)),
        out_shape=jax.ShapeDtypeStruct((nb, n), F32),
        compiler_params=_cparams(("arbitrary",)),
        name="ada",
    )(cb, w_ada, b_ada.reshape(1, n))


def _modulated_norm(x, g, sc, sh):
    ms = jnp.mean(x * x, axis=-1, keepdims=True)
    return (x * lax.rsqrt(ms + RMS_EPS) * g) * (1.0 + sc) + sh


def _front_kernel(x_ref, g_ref, sc_ref, sh_ref, w_ref, cw_ref, cb_ref, gg_ref, gb_ref,
                  oc_ref, p_ref, ush_ref, acc_ref, carry_ref, *, tn):
    tm = x_ref.shape[0]
    cw = oc_ref.shape[1]
    nsh = ush_ref.shape[1]
    n_rest = w_ref.shape[1] - 2 * cw
    first = pl.program_id(1) == 0
    h = _modulated_norm(x_ref[...], g_ref[...], sc_ref[...], sh_ref[...]).astype(BF16)

    @pl.when(first)
    def _():
        carry_ref[...] = jnp.zeros_like(carry_ref)

    ush_ref[0, 0:HALO, :] = carry_ref[...]
    for j in range(cw // tn):
        sl = slice(j * tn, (j + 1) * tn)
        val = _dot(h, w_ref[:, sl])
        gate = _dot(h, w_ref[:, cw + j * tn:cw + (j + 1) * tn])
        ush_ref[0, HALO:HALO + tm, sl] = val * _sigmoid(gate)
    carry_ref[...] = ush_ref[0, tm:tm + HALO, :]
    for phi in range(1, SUBLANES):
        ush_ref[phi, 0:nsh - SUBLANES, :] = ush_ref[0, phi:phi + nsh - SUBLANES, :]

    for lo in range(0, n_rest, tn):
        hi = min(lo + tn, n_rest)
        p_ref[:, lo:hi] = _dot(h, w_ref[:, 2 * cw + lo:2 * cw + hi]).astype(p_ref.dtype)
    if p_ref.shape[1] > n_rest:
        p_ref[:, n_rest:] = jnp.zeros((tm, p_ref.shape[1] - n_rest), p_ref.dtype)

    rb = 64
    base = HALO - (CONV_KERNEL - 1)
    for c in range(cw // LANES):
        sl = slice(c * LANES, (c + 1) * LANES)
        for r in range(tm // rb):
            acc = jnp.zeros((rb, LANES), F32) + cb_ref[:, sl]
            for j in range(CONV_KERNEL):
                q, phi = divmod(base + j, SUBLANES)
                lo = r * rb + q * SUBLANES
                acc = acc + cw_ref[j:j + 1, sl] * ush_ref[phi, lo:lo + rb, sl]
            acc_ref[r * rb:(r + 1) * rb, sl] = acc

    ones_bd = _block_diag_ones()
    u = acc_ref[...]
    mean = _head_sum(u, ones_bd) * (1.0 / HEAD_DIM)
    d = u - mean
    var = _head_sum(d * d, ones_bd) * (1.0 / HEAD_DIM)
    y = d * lax.rsqrt(var + CONV_GN_EPS) * gg_ref[...] + gb_ref[...]
    oc_ref[...] = (y * _sigmoid(y)).astype(oc_ref.dtype)


def _front(x, g, mod4, w, conv_w, conv_b, gn_g, gn_b, cw, n_pad, tm, tn):
    nb, t, d = x.shape
    n = w.shape[1]
    row = lambda v: v.reshape(1, cw)
    vec = pl.BlockSpec((1, cw), lambda b, i: (0, 0))
    return pl.pallas_call(
        functools.partial(_front_kernel, tn=tn),
        grid=(nb, t // tm),
        in_specs=[pl.BlockSpec((None, tm, d), lambda b, i: (b, i, 0)),
                  pl.BlockSpec((1, d), lambda b, i: (0, 0)),
                  pl.BlockSpec((None, None, 1, d), lambda b, i: (b, 1, 0, 0)),
                  pl.BlockSpec((None, None, 1, d), lambda b, i: (b, 0, 0, 0)),
                  pl.BlockSpec((d, n), lambda b, i: (0, 0), pipeline_mode=pl.Buffered(1)),
                  pl.BlockSpec((CONV_KERNEL, cw), lambda b, i: (0, 0)), vec, vec, vec],
        out_specs=[pl.BlockSpec((None, tm, cw), lambda b, i: (b, i, 0)),
                   pl.BlockSpec((None, tm, n_pad), lambda b, i: (b, i, 0))],
        out_shape=[jax.ShapeDtypeStruct((nb, t, cw), BF16),
                   jax.ShapeDtypeStruct((nb, t, n_pad), BF16)],
        scratch_shapes=[pltpu.VMEM((SUBLANES, tm + HALO, cw), F32), pltpu.VMEM((tm, cw), F32),
                        pltpu.VMEM((HALO, cw), F32)],
        compiler_params=_cparams(("arbitrary", "arbitrary")),
        name="front",
    )(x, g.reshape(1, d), mod4, mod4, w, conv_w, row(conv_b), row(gn_g), row(gn_b))


def _prep_kernel(pr_ref, pk_ref, pv_ref, pl_ref, hr_ref, hk_ref, hv_ref, hl_ref,
                 mu_rkv_ref, mu_lora_ref,
                 w0_ref, ww2_ref, a0_ref, aw2_ref, gw2_ref, kk_ref, ka_ref, rk_ref,
                 r_ref, k_ref, v_ref, an_ref, b_ref, lw_ref, g_ref, bonus_ref, *, win_w, win_a, win_g):
    tm, rw = r_ref.shape
    first = pl.program_id(1) == 0
    rows = lax.broadcasted_iota(jnp.int32, (tm, 1), 0)

    def shifted(cur_ref, halo_ref, mu):
        cur = cur_ref[...].astype(F32)
        last = jnp.where(first, 0.0, halo_ref[BF16_ROWS - 1:BF16_ROWS, :].astype(F32))
        prev = jnp.where(rows == 0, last, pltpu.roll(cur, 1, axis=0))
        return cur + (prev - cur) * mu

    r = shifted(pr_ref, hr_ref, mu_rkv_ref[:, :rw])
    k = shifted(pk_ref, hk_ref, mu_rkv_ref[:, rw:2 * rw])
    v = shifted(pv_ref, hv_ref, mu_rkv_ref[:, 2 * rw:])
    ql = shifted(pl_ref, hl_ref, mu_lora_ref[...])
    xw, xa, xg = (ql[:, lo:hi] for lo, hi in (win_w, win_a, win_g))

    z = w0_ref[...] + _dot(jnp.tanh(xw).astype(BF16), ww2_ref[...])
    lw_ref[...] = (-math.exp(-0.5)) * _sigmoid(z)
    a = _sigmoid(a0_ref[...] + _dot(xa.astype(BF16), aw2_ref[...]))
    g_ref[...] = _dot(_sigmoid(xg).astype(BF16), gw2_ref[...]).astype(g_ref.dtype)

    ones_bd = _block_diag_ones()
    kk = k * kk_ref[...]
    ss = _head_sum(kk * kk, ones_bd)
    kkn = kk * lax.rsqrt(jnp.maximum(ss, 1e-24))
    kmod = k * (1.0 + (a - 1.0) * ka_ref[...])
    r_ref[...] = r.astype(r_ref.dtype)
    k_ref[...] = kmod.astype(k_ref.dtype)
    v_ref[...] = v.astype(v_ref.dtype)
    an_ref[...] = (-kkn).astype(an_ref.dtype)
    b_ref[...] = (kkn * a).astype(b_ref.dtype)
    bonus_ref[...] = (_head_sum(r * kmod * rk_ref[...], ones_bd) * v).astype(bonus_ref.dtype)


def _lane_window(lo, hi):
    return lo // LANES * LANES, -(-hi // LANES) * LANES


def _window_weight(w, lo, hi):
    a0, a1 = _lane_window(lo, hi)
    out = jnp.zeros((a1 - a0, w.shape[1]), w.dtype)
    return out.at[lo - a0:hi - a0].set(w)


def _prep(p, mu, w0, w_w2, a0, a_w2, g_w2, k_k, k_a, r_k, rw, tm):
    nb, t, n_pad = p.shape
    nrkv = 3 * rw
    nl = n_pad - nrkv
    dl, al, gl = w_w2.shape[0], a_w2.shape[0], g_w2.shape[0]
    assert nrkv % nl == 0
    r_blk = 0
    lora_blk = nrkv // nl
    wins = (_lane_window(0, dl), _lane_window(dl, dl + al), _lane_window(dl + al, dl + al + gl))
    ww2 = _window_weight(w_w2, 0, dl).astype(BF16)
    aw2 = _window_weight(a_w2, dl, dl + al).astype(BF16)
    gw2 = _window_weight(g_w2, dl + al, dl + al + gl).astype(BF16)
    mu_rkv = mu[:nrkv]
    mu_lora = _pad_cols(mu[nrkv:], nl)
    hb = tm // BF16_ROWS
    row = lambda v_: v_.reshape(1, -1)
    full = lambda shape: pl.BlockSpec(shape, lambda b, i: (0,) * len(shape))
    cur = lambda w, blk: pl.BlockSpec((None, tm, w), lambda b, i: (b, i, blk))
    halo = lambda w, blk: pl.BlockSpec((None, BF16_ROWS, w),
                                       lambda b, i: (b, jnp.maximum(i * hb - 1, 0), blk))
    out_spec = pl.BlockSpec((None, tm, rw), lambda b, i: (b, i, 0))
    bf = jax.ShapeDtypeStruct((nb, t, rw), BF16)
    return pl.pallas_call(
        functools.partial(_prep_kernel, win_w=wins[0], win_a=wins[1], win_g=wins[2]),
        grid=(nb, t // tm),
        in_specs=[cur(rw, r_blk), cur(rw, r_blk + 1), cur(rw, r_blk + 2), cur(nl, lora_blk),
                  halo(rw, r_blk), halo(rw, r_blk + 1), halo(rw, r_blk + 2), halo(nl, lora_blk),
                  full((1, nrkv)), full((1, nl)),
                  full((1, rw)), full(ww2.shape), full((1, rw)), full(aw2.shape),
                  full(gw2.shape), full((1, rw)), full((1, rw)), full((1, rw))],
        out_specs=[out_spec] * 8,
        out_shape=[bf, bf, bf, bf, bf, jax.ShapeDtypeStruct((nb, t, rw), F32), bf, bf],
        compiler_params=_cparams(("arbitrary", "arbitrary")),
        name="prep",
    )(p, p, p, p, p, p, p, p, row(mu_rkv), row(mu_lora), row(w0), ww2, row(a0), aw2, gw2,
      row(k_k), row(k_a), row(r_k))


def _wkv_kernel(r_ref, k_ref, v_ref, an_ref, b_ref, lw_ref, y_ref, s_ref):
    nb, tb, rw = r_ref.shape
    ngroups = rw // GROUP
    reps = GROUP // CHUNK
    streams = [(bi, g) for bi in range(nb) for g in range(ngroups)]

    @pl.when(pl.program_id(0) == 0)
    def _():
        s_ref[...] = jnp.zeros_like(s_ref)

    row = lax.broadcasted_iota(jnp.int32, (GROUP, GROUP), 0)
    col = lax.broadcasted_iota(jnp.int32, (GROUP, GROUP), 1)
    bd = (row // HEAD_DIM) == (col // HEAD_DIM)
    bd16 = (row // HEAD_DIM).astype(F32).astype(BF16) == (col // HEAD_DIM).astype(F32).astype(BF16)
    t_idx = lax.broadcasted_iota(jnp.int32, (CHUNK, GROUP), 0)
    j_idx = lax.broadcasted_iota(jnp.int32, (CHUNK, GROUP), 1) % CHUNK
    tj_strict = t_idx > j_idx
    tj_incl = t_idx >= j_idx
    ident = jnp.where(t_idx == j_idx, 1.0, 0.0)
    tr = lax.broadcasted_iota(jnp.int32, (CHUNK, CHUNK), 0)
    tc = lax.broadcasted_iota(jnp.int32, (CHUNK, CHUNK), 1)
    ltri = jnp.where(tr >= tc, 1.0, 0.0).astype(BF16)
    nlev = int(math.log2(CHUNK)) - 1

    def expand(xc):
        xb = xc.astype(BF16)
        return jnp.where(bd16, jnp.concatenate([xb] * reps, axis=0), jnp.zeros((), BF16))

    def stack(a, b):
        return jnp.concatenate([a, b], axis=0).astype(BF16)

    def chunk_body(c, carry):
        rs = pl.ds(pl.multiple_of(c * CHUNK, CHUNK), CHUNK)
        ns = len(streams)
        vg, ar, kb, vexp, kbe, p_end = [], [], [], [], [], []
        for bi, g in streams:
            ls = slice(g * GROUP, (g + 1) * GROUP)
            lw = lw_ref[bi, rs, ls]
            hi, lo = _split2(lw)
            cum = _dot(ltri, hi) + _dot(ltri, lo)
            tail = cum[CHUNK - 1:CHUNK, :]
            e_neg = jnp.exp(-cum)
            e_tail = jnp.exp(tail - cum)
            k_ = k_ref[bi, rs, ls].astype(F32)
            b_ = b_ref[bi, rs, ls].astype(F32)
            v_ = v_ref[bi, rs, ls].astype(F32)
            vg.append(v_)
            ar.append(stack(an_ref[bi, rs, ls].astype(F32) * jnp.exp(cum - lw),
                            r_ref[bi, rs, ls].astype(F32) * jnp.exp(cum)))
            kb.append(jnp.concatenate([expand(k_ * e_neg), expand(b_ * e_neg)], axis=0))
            vexp.append(expand(v_))
            kbe.append(stack(k_ * e_tail, b_ * e_tail))
            p_end.append(jnp.exp(tail))

        amat = [_dot_nt(ar[i], kb[i]) for i in range(ns)]
        a_kc = [stack(jnp.where(tj_strict, amat[i][:CHUNK, :GROUP], 0.0),
                      jnp.where(tj_incl, amat[i][CHUNK:, :GROUP], 0.0)) for i in range(ns)]
        a_rb = [jnp.where(tj_incl, amat[i][CHUNK:, GROUP:], 0.0).astype(BF16) for i in range(ns)]
        xp = [jnp.where(tj_strict, amat[i][:CHUNK, GROUP:], 0.0) for i in range(ns)]

        t_inv = [ident + xp[i] for i in range(ns)]
        xp = [_dot(xp[i].astype(BF16), expand(xp[i])) for i in range(ns)]
        for lev in range(nlev):
            if lev < nlev - 1:
                both = [_dot(stack(xp[i], t_inv[i]), expand(xp[i])) for i in range(ns)]
                xp = [both[i][:CHUNK] for i in range(ns)]
                t_inv = [t_inv[i] + both[i][CHUNK:] for i in range(ns)]
            else:
                t_inv = [t_inv[i] + _dot(t_inv[i].astype(BF16), expand(xp[i])) for i in range(ns)]

        s_old = [s_ref[i] for i in range(ns)]
        ars = [_dot_nt(ar[i], s_old[i].astype(BF16)) for i in range(ns)]
        akv = [_dot(a_kc[i], vexp[i]) for i in range(ns)]
        z = [ars[i][:CHUNK] + akv[i][:CHUNK] for i in range(ns)]
        u = [_dot(t_inv[i].astype(BF16), expand(z[i])) for i in range(ns)]
        y = [ars[i][CHUNK:] + akv[i][CHUNK:] + _dot(a_rb[i], expand(u[i])) for i in range(ns)]
        ds = [_dot_tn(stack(vg[i], u[i]), kbe[i]) for i in range(ns)]
        for i, (bi, g) in enumerate(streams):
            y_ref[bi, rs, g * GROUP:(g + 1) * GROUP] = y[i]
            s_ref[i] = s_old[i] * p_end[i] + jnp.where(bd, ds[i], 0.0)
        return carry

    lax.fori_loop(0, tb // CHUNK, chunk_body, 0)


def _wkv(r, k, v, an, b, lw, tb):
    nb, t, rw = r.shape
    spec = pl.BlockSpec((nb, tb, rw), lambda i: (0, i, 0))
    return pl.pallas_call(
        _wkv_kernel,
        grid=(t // tb,),
        in_specs=[spec] * 6,
        out_specs=spec,
        out_shape=jax.ShapeDtypeStruct((nb, t, rw), F32),
        scratch_shapes=[pltpu.VMEM((nb * (rw // GROUP), GROUP, GROUP), F32)],
        compiler_params=_cparams(("arbitrary",)),
        name="wkv",
    )(r, k, v, an, b, lw)


def _outproj_kernel(oc_ref, y_ref, bonus_ref, g_ref, lg_ref, lb_ref, wa_ref, wb_ref, x_ref, g1_ref,
                    o_ref, *, tn):
    nchunks = o_ref.shape[1] // tn
    oc = oc_ref[...]
    acc = [_dot(oc, wa_ref[:, j * tn:(j + 1) * tn]) for j in range(nchunks)]
    ones_bd = _block_diag_ones()
    y = y_ref[...]
    mean = _head_sum(y, ones_bd) * (1.0 / HEAD_DIM)
    d = y - mean
    var = _head_sum(d * d, ones_bd) * (1.0 / HEAD_DIM)
    yn = d * lax.rsqrt(var + RWKV_GN_EPS) * lg_ref[...] + lb_ref[...]
    orw = ((yn + bonus_ref[...].astype(F32)) * g_ref[...].astype(F32)).astype(BF16)
    for j in range(nchunks):
        sl = slice(j * tn, (j + 1) * tn)
        o_ref[:, sl] = x_ref[:, sl] + g1_ref[:, sl] * (acc[j] + _dot(orw, wb_ref[:, sl]))


def _outproj(oc, y, bonus, g, lnx_g, lnx_b, w_out2, x, mod4, tm, tn):
    nb, t, d = x.shape
    cw, rw = oc.shape[2], y.shape[2]
    rows = lambda w: pl.BlockSpec((None, tm, w), lambda b, i: (b, i, 0))
    vec = pl.BlockSpec((1, rw), lambda b, i: (0, 0))
    return pl.pallas_call(
        functools.partial(_outproj_kernel, tn=tn),
        grid=(nb, t // tm),
        in_specs=[rows(cw), rows(rw), rows(rw), rows(rw), vec, vec,
                  pl.BlockSpec((None, cw, d), lambda b, i: (0, 0, 0), pipeline_mode=pl.Buffered(1)),
                  pl.BlockSpec((None, rw, d), lambda b, i: (1, 0, 0), pipeline_mode=pl.Buffered(1)),
                  rows(d),
                  pl.BlockSpec((None, None, 1, d), lambda b, i: (b, 2, 0, 0))],
        out_specs=rows(d),
        out_shape=jax.ShapeDtypeStruct((nb, t, d), F32),
        compiler_params=_cparams(("arbitrary", "arbitrary")),
        name="outproj",
    )(oc, y, bonus, g, lnx_g.reshape(1, rw), lnx_b.reshape(1, rw), w_out2, w_out2, x, mod4)


FFN_NORM_PARTS = 8


def _ffn_kernel(x_ref, xn_ref, g_ref, sc_ref, sh_ref, scn_ref, shn_ref, gate_ref, gf_ref,
                w1_ref, w3_ref, w2_ref, o_ref, h_ref, acc_ref, *, final_norm):
    b, i, j = pl.program_id(0), pl.program_id(1), pl.program_id(2)
    nj = pl.num_programs(2)
    tm = x_ref.shape[0]
    part = tm // FFN_NORM_PARTS
    lin = b * pl.num_programs(1) + i
    slot = lin % 2

    @pl.when((lin == 0) & (j == 0))
    def _():
        h_ref[0] = _modulated_norm(x_ref[...], g_ref[...], sc_ref[...], sh_ref[...]).astype(BF16)

    @pl.when(j == 0)
    def _():
        acc_ref[...] = jnp.zeros_like(acc_ref)

    h = h_ref[slot]
    a = _dot(h, w1_ref[...])
    bb = _dot(h, w3_ref[...])
    act = (a * _sigmoid(a) * bb).astype(BF16)

    rs = pl.ds(pl.multiple_of(jnp.minimum(j, FFN_NORM_PARTS - 1) * part, part), part)
    h_ref[1 - slot, rs, :] = _modulated_norm(xn_ref[rs, :], g_ref[...], scn_ref[...],
                                             shn_ref[...]).astype(BF16)

    acc_ref[...] += _dot(act, w2_ref[...])

    @pl.when(j == nj - 1)
    def _():
        xo = x_ref[...] + gate_ref[...] * acc_ref[...]
        if final_norm:
            ms = jnp.mean(xo * xo, axis=-1, keepdims=True)
            xo = xo * lax.rsqrt(ms + RMS_EPS) * gf_ref[...]
        o_ref[...] = xo


def _ffn(x1, g, mod4, gf, w1, w3, w2, tm, tf, final_norm):
    nb, t, d = x1.shape
    f = w1.shape[1]
    nt = t // tm
    assert f // tf >= FFN_NORM_PARTS and tm % (FFN_NORM_PARTS * BF16_ROWS) == 0

    def nxt(b, i):
        lin = jnp.minimum(b * nt + i + 1, nb * nt - 1)
        return lin // nt, lin % nt

    modspec = lambda comp: pl.BlockSpec((None, None, 1, d), lambda b, i, j: (b, comp, 0, 0))
    modnext = lambda comp: pl.BlockSpec((None, None, 1, d), lambda b, i, j: (nxt(b, i)[0], comp, 0, 0))
    vec = pl.BlockSpec((1, d), lambda b, i, j: (0, 0))
    return pl.pallas_call(
        functools.partial(_ffn_kernel, final_norm=final_norm),
        grid=(nb, nt, f // tf),
        in_specs=[pl.BlockSpec((None, tm, d), lambda b, i, j: (b, i, 0)),
                  pl.BlockSpec((None, tm, d), lambda b, i, j: nxt(b, i) + (0,)),
                  vec, modspec(4), modspec(3), modnext(4), modnext(3), modspec(5), vec,
                  pl.BlockSpec((d, tf), lambda b, i, j: (0, j)),
                  pl.BlockSpec((d, tf), lambda b, i, j: (0, j)),
                  pl.BlockSpec((tf, d), lambda b, i, j: (j, 0))],
        out_specs=pl.BlockSpec((None, tm, d), lambda b, i, j: (b, i, 0)),
        out_shape=jax.ShapeDtypeStruct((nb, t, d), F32),
        scratch_shapes=[pltpu.VMEM((2, tm, d), BF16), pltpu.VMEM((tm, d), F32)],
        compiler_params=_cparams(("arbitrary", "arbitrary", "arbitrary")),
        name="ffn",
    )(x1, x1, g.reshape(1, d), mod4, mod4, mod4, mod4, mod4, gf.reshape(1, d),
      w1.astype(BF16), w3.astype(BF16), w2.astype(BF16))


def _pad_cols(w, n):
    return jnp.concatenate([w, jnp.zeros(w.shape[:-1] + (n - w.shape[-1],), w.dtype)], axis=-1)


def _tile(n, pref):
    return pref if n % pref == 0 else n


def kernel(x, c, w_ada, b_ada, norm1_g, w_in, tshift_mu, conv_w, conv_b, conv_gn_g, conv_gn_b,
           w0, w_w2, a0, a_w2, g_w2, k_k, k_a, r_k, lnx_g, lnx_b, w_out, norm2_g,
           w_ff1, w_ff3, w_ff2, norm_f_g):
    nb, t, d = x.shape
    depth = w_in.shape[0]
    cw = conv_w.shape[2]
    rw = w0.shape[1]
    n_rest = w_in.shape[2] - 2 * cw
    n_pad = -(-n_rest // LANES) * LANES
    assert rw % GROUP == 0 and cw % GROUP == 0

    for l in range(depth):
        mod = _ada(c, w_ada[l], b_ada[l])
        mod4 = mod.reshape(nb, N_MOD, 1, d)
        oc, p = _front(x, norm1_g[l], mod4, w_in[l].astype(BF16), conv_w[l], conv_b[l],
                       conv_gn_g[l], conv_gn_b[l], cw, n_pad, _tile(t, 256), 512)
        r, k, v, an, b, lw, g, bonus = _prep(p, tshift_mu[l], w0[l], w_w2[l], a0[l], a_w2[l], g_w2[l],
                                             k_k[l], k_a[l], r_k[l].reshape(-1), rw, _tile(t, 256))
        y = _wkv(r, k, v, an, b, lw, _tile(t, 256))
        x = _outproj(oc, y, bonus, g, lnx_g[l], lnx_b[l], w_out[l].astype(BF16).reshape(2, cw, d),
                     x, mod4, _tile(t, 512), 512)
        x = _ffn(x, norm2_g[l], mod4, norm_f_g, w_ff1[l], w_ff3[l], w_ff2[l],
                 _tile(t, 512), 512, final_norm=(l == depth - 1))
    return x
```

```python
import functools
import math

import jax
import jax.numpy as jnp
from jax import lax
from jax.experimental import pallas as pl
from jax.experimental.pallas import tpu as pltpu

F32 = jnp.float32
BF16 = jnp.bfloat16

HEAD_DIM = 64
CONV_KERNEL = 31
RMS_EPS = 1e-6
CONV_GN_EPS = 1e-5
RWKV_GN_EPS = 64e-5
N_MOD = 6

LANES = 128
SUBLANES = 8
BF16_ROWS = 16
MXU_DIM = 256
GROUP = MXU_DIM
CHUNK = 64
HALO = 32
VMEM_LIMIT = 52 * 1024 * 1024


def _cparams(sem):
    return pltpu.CompilerParams(dimension_semantics=sem, vmem_limit_bytes=VMEM_LIMIT)


def _dot(a, b):
    return jnp.dot(a, b, preferred_element_type=F32)


def _dot_nt(a, b):
    return lax.dot_general(a, b, (((1,), (1,)), ((), ())), preferred_element_type=F32)


def _dot_tn(a, b):
    return lax.dot_general(a, b, (((0,), (0,)), ((), ())), preferred_element_type=F32)


def _split2(x):
    hi = x.astype(BF16)
    lo = (x - hi.astype(F32)).astype(BF16)
    return hi, lo


def _sigmoid(x):
    return 1.0 / (1.0 + jnp.exp(-x))


def _head_sum(x, ones_bd):
    outs = [_dot(x[:, g * GROUP:(g + 1) * GROUP].astype(BF16), ones_bd)
            for g in range(x.shape[1] // GROUP)]
    return outs[0] if len(outs) == 1 else jnp.concatenate(outs, axis=1)


def _block_diag_ones():
    r = lax.broadcasted_iota(jnp.int32, (GROUP, GROUP), 0) // HEAD_DIM
    c = lax.broadcasted_iota(jnp.int32, (GROUP, GROUP), 1) // HEAD_DIM
    return jnp.where(r == c, 1.0, 0.0).astype(BF16)


def _ada_kernel(cb_ref, w_ref, b_ref, o_ref):
    nb = cb_ref.shape[0]
    tn = w_ref.shape[1]
    for b in range(nb):
        cb = cb_ref[b]
        sc = cb * _sigmoid(cb)
        for j in range(tn // LANES):
            sl = slice(j * LANES, (j + 1) * LANES)
            s = jnp.sum(w_ref[:, sl] * sc, axis=0, keepdims=True)
            o_ref[b:b + 1, sl] = s + b_ref[:, sl]


def _ada(c, w_ada, b_ada):
    nb, d = c.shape
    n = w_ada.shape[1]
    tn = 1024
    cb = jnp.broadcast_to(c[:, :, None], (nb, d, LANES))
    return pl.pallas_call(
        _ada_kernel,
        grid=(n // tn,),
        in_specs=[pl.BlockSpec((nb, d, LANES), lambda j: (0, 0, 0)),
                  pl.BlockSpec((d, tn), lambda j: (0, j)),
                  pl.BlockSpec((1, tn), lambda j: (0, j))],
        out_specs=pl.BlockSpec((nb, tn), lambda j: (0, j)),
        out_shape=jax.ShapeDtypeStruct((nb, n), F32),
        compiler_params=_cparams(("arbitrary",)),
        name="ada",
    )(cb, w_ada, b_ada.reshape(1, n))


def _modulated_norm(x, g, sc, sh):
    ms = jnp.mean(x * x, axis=-1, keepdims=True)
    return (x * lax.rsqrt(ms + RMS_EPS) * g) * (1.0 + sc) + sh


def _front_kernel(x_ref, g_ref, sc_ref, sh_ref, w_ref, cw_ref, cb_ref, gg_ref, gb_ref,
                  oc_ref, p_ref, ush_ref, acc_ref, carry_ref, *, tn):
    tm = x_ref.shape[0]
    cw = oc_ref.shape[1]
    nsh = ush_ref.shape[1]
    n_rest = w_ref.shape[1] - 2 * cw
    first = pl.program_id(1) == 0
    h = _modulated_norm(x_ref[...], g_ref[...], sc_ref[...], sh_ref[...]).astype(BF16)

    @pl.when(first)
    def _():
        carry_ref[...] = jnp.zeros_like(carry_ref)

    ush_ref[0, 0:HALO, :] = carry_ref[...]
    for j in range(cw // tn):
        sl = slice(j * tn, (j + 1) * tn)
        val = _dot(h, w_ref[:, sl])
        gate = _dot(h, w_ref[:, cw + j * tn:cw + (j + 1) * tn])
        ush_ref[0, HALO:HALO + tm, sl] = val * _sigmoid(gate)
    carry_ref[...] = ush_ref[0, tm:tm + HALO, :]
    for phi in range(1, SUBLANES):
        ush_ref[phi, 0:nsh - SUBLANES, :] = ush_ref[0, phi:phi + nsh - SUBLANES, :]

    for lo in range(0, n_rest, tn):
        hi = min(lo + tn, n_rest)
        p_ref[:, lo:hi] = _dot(h, w_ref[:, 2 * cw + lo:2 * cw + hi]).astype(p_ref.dtype)
    if p_ref.shape[1] > n_rest:
        p_ref[:, n_rest:] = jnp.zeros((tm, p_ref.shape[1] - n_rest), p_ref.dtype)

    rb = 64
    base = HALO - (CONV_KERNEL - 1)
    for c in range(cw // LANES):
        sl = slice(c * LANES, (c + 1) * LANES)
        for r in range(tm // rb):
            acc = jnp.zeros((rb, LANES), F32) + cb_ref[:, sl]
            for j in range(CONV_KERNEL):
                q, phi = divmod(base + j, SUBLANES)
                lo = r * rb + q * SUBLANES
                acc = acc + cw_ref[j:j + 1, sl] * ush_ref[phi, lo:lo + rb, sl]
            acc_ref[r * rb:(r + 1) * rb, sl] = acc

    ones_bd = _block_diag_ones()
    u = acc_ref[...]
    mean = _head_sum(u, ones_bd) * (1.0 / HEAD_DIM)
    d = u - mean
    var = _head_sum(d * d, ones_bd) * (1.0 / HEAD_DIM)
    y = d * lax.rsqrt(var + CONV_GN_EPS) * gg_ref[...] + gb_ref[...]
    oc_ref[...] = (y * _sigmoid(y)).astype(oc_ref.dtype)


def _front(x, g, mod4, w, conv_w, conv_b, gn_g, gn_b, cw, n_pad, tm, tn):
    nb, t, d = x.shape
    n = w.shape[1]
    row = lambda v: v.reshape(1, cw)
    vec = pl.BlockSpec((1, cw), lambda b, i: (0, 0))
    return pl.pallas_call(
        functools.partial(_front_kernel, tn=tn),
        grid=(nb, t // tm),
        in_specs=[pl.BlockSpec((None, tm, d), lambda b, i: (b, i, 0)),
                  pl.BlockSpec((1, d), lambda b, i: (0, 0)),
                  pl.BlockSpec((None, None, 1, d), lambda b, i: (b, 1, 0, 0)),
                  pl.BlockSpec((None, None, 1, d), lambda b, i: (b, 0, 0, 0)),
                  pl.BlockSpec((d, n), lambda b, i: (0, 0), pipeline_mode=pl.Buffered(1)),
                  pl.BlockSpec((CONV_KERNEL, cw), lambda b, i: (0, 0)), vec, vec, vec],
        out_specs=[pl.BlockSpec((None, tm, cw), lambda b, i: (b, i, 0)),
                   pl.BlockSpec((None, tm, n_pad), lambda b, i: (b, i, 0))],
        out_shape=[jax.ShapeDtypeStruct((nb, t, cw), BF16),
                   jax.ShapeDtypeStruct((nb, t, n_pad), BF16)],
        scratch_shapes=[pltpu.VMEM((SUBLANES, tm + HALO, cw), F32), pltpu.VMEM((tm, cw), F32),
                        pltpu.VMEM((HALO, cw), F32)],
        compiler_params=_cparams(("arbitrary", "arbitrary")),
        name="front",
    )(x, g.reshape(1, d), mod4, mod4, w, conv_w, row(conv_b), row(gn_g), row(gn_b))


def _prep_kernel(pr_ref, pk_ref, pv_ref, pl_ref, hr_ref, hk_ref, hv_ref, hl_ref,
                 mu_rkv_ref, mu_lora_ref,
                 w0_ref, ww2_ref, a0_ref, aw2_ref, gw2_ref, kk_ref, ka_ref, rk_ref,
                 r_ref, k_ref, v_ref, an_ref, b_ref, lw_ref, g_ref, bonus_ref, *, win_w, win_a, win_g):
    tm, rw = r_ref.shape
    first = pl.program_id(1) == 0
    rows = lax.broadcasted_iota(jnp.int32, (tm, 1), 0)

    def shifted(cur_ref, halo_ref, mu):
        cur = cur_ref[...].astype(F32)
        last = jnp.where(first, 0.0, halo_ref[BF16_ROWS - 1:BF16_ROWS, :].astype(F32))
        prev = jnp.where(rows == 0, last, pltpu.roll(cur, 1, axis=0))
        return cur + (prev - cur) * mu

    r = shifted(pr_ref, hr_ref, mu_rkv_ref[:, :rw])
    k = shifted(pk_ref, hk_ref, mu_rkv_ref[:, rw:2 * rw])
    v = shifted(pv_ref, hv_ref, mu_rkv_ref[:, 2 * rw:])
    ql = shifted(pl_ref, hl_ref, mu_lora_ref[...])
    xw, xa, xg = (ql[:, lo:hi] for lo, hi in (win_w, win_a, win_g))

    z = w0_ref[...] + _dot(jnp.tanh(xw).astype(BF16), ww2_ref[...])
    lw_ref[...] = (-math.exp(-0.5)) * _sigmoid(z)
    a = _sigmoid(a0_ref[...] + _dot(xa.astype(BF16), aw2_ref[...]))
    g_ref[...] = _dot(_sigmoid(xg).astype(BF16), gw2_ref[...]).astype(g_ref.dtype)

    ones_bd = _block_diag_ones()
    kk = k * kk_ref[...]
    ss = _head_sum(kk * kk, ones_bd)
    kkn = kk * lax.rsqrt(jnp.maximum(ss, 1e-24))
    kmod = k * (1.0 + (a - 1.0) * ka_ref[...])
    r_ref[...] = r.astype(r_ref.dtype)
    k_ref[...] = kmod.astype(k_ref.dtype)
    v_ref[...] = v.astype(v_ref.dtype)
    an_ref[...] = (-kkn).astype(an_ref.dtype)
    b_ref[...] = (kkn * a).astype(b_ref.dtype)
    bonus_ref[...] = (_head_sum(r * kmod * rk_ref[...], ones_bd) * v).astype(bonus_ref.dtype)


def _lane_window(lo, hi):
    return lo // LANES * LANES, -(-hi // LANES) * LANES


def _window_weight(w, lo, hi):
    a0, a1 = _lane_window(lo, hi)
    out = jnp.zeros((a1 - a0, w.shape[1]), w.dtype)
    return out.at[lo - a0:hi - a0].set(w)


def _prep(p, mu, w0, w_w2, a0, a_w2, g_w2, k_k, k_a, r_k, rw, tm):
    nb, t, n_pad = p.shape
    nrkv = 3 * rw
    nl = n_pad - nrkv
    dl, al, gl = w_w2.shape[0], a_w2.shape[0], g_w2.shape[0]
    assert nrkv % nl == 0
    r_blk = 0
    lora_blk = nrkv // nl
    wins = (_lane_window(0, dl), _lane_window(dl, dl + al), _lane_window(dl + al, dl + al + gl))
    ww2 = _window_weight(w_w2, 0, dl).astype(BF16)
    aw2 = _window_weight(a_w2, dl, dl + al).astype(BF16)
    gw2 = _window_weight(g_w2, dl + al, dl + al + gl).astype(BF16)
    mu_rkv = mu[:nrkv]
    mu_lora = _pad_cols(mu[nrkv:], nl)
    hb = tm // BF16_ROWS
    row = lambda v_: v_.reshape(1, -1)
    full = lambda shape: pl.BlockSpec(shape, lambda b, i: (0,) * len(shape))
    cur = lambda w, blk: pl.BlockSpec((None, tm, w), lambda b, i: (b, i, blk))
    halo = lambda w, blk: pl.BlockSpec((None, BF16_ROWS, w),
                                       lambda b, i: (b, jnp.maximum(i * hb - 1, 0), blk))
    out_spec = pl.BlockSpec((None, tm, rw), lambda b, i: (b, i, 0))
    bf = jax.ShapeDtypeStruct((nb, t, rw), BF16)
    return pl.pallas_call(
        functools.partial(_prep_kernel, win_w=wins[0], win_a=wins[1], win_g=wins[2]),
        grid=(nb, t // tm),
        in_specs=[cur(rw, r_blk), cur(rw, r_blk + 1), cur(rw, r_blk + 2), cur(nl, lora_blk),
                  halo(rw, r_blk), halo(rw, r_blk + 1), halo(rw, r_blk + 2), halo(nl, lora_blk),
                  full((1, nrkv)), full((1, nl)),
                  full((1, rw)), full(ww2.shape), full((1, rw)), full(aw2.shape),
                  full(gw2.shape), full((1, rw)), full((1, rw)), full((1, rw))],
        out_specs=[out_spec] * 8,
        out_shape=[bf, bf, bf, bf, bf, jax.ShapeDtypeStruct((nb, t, rw), F32), bf, bf],
        compiler_params=_cparams(("arbitrary", "arbitrary")),
        name="prep",
    )(p, p, p, p, p, p, p, p, row(mu_rkv), row(mu_lora), row(w0), ww2, row(a0), aw2, gw2,
      row(k_k), row(k_a), row(r_k))


def _wkv_kernel(r_ref, k_ref, v_ref, an_ref, b_ref, lw_ref, y_ref, s_ref):
    nb, tb, rw = r_ref.shape
    ngroups = rw // GROUP
    reps = GROUP // CHUNK
    streams = [(bi, g) for bi in range(nb) for g in range(ngroups)]

    @pl.when(pl.program_id(0) == 0)
    def _():
        s_ref[...] = jnp.zeros_like(s_ref)

    row = lax.broadcasted_iota(jnp.int32, (GROUP, GROUP), 0)
    col = lax.broadcasted_iota(jnp.int32, (GROUP, GROUP), 1)
    bd = (row // HEAD_DIM) == (col // HEAD_DIM)
    bd16 = (row // HEAD_DIM).astype(F32).astype(BF16) == (col // HEAD_DIM).astype(F32).astype(BF16)
    t_idx = lax.broadcasted_iota(jnp.int32, (CHUNK, GROUP), 0)
    j_idx = lax.broadcasted_iota(jnp.int32, (CHUNK, GROUP), 1) % CHUNK
    tj_strict = t_idx > j_idx
    tj_incl = t_idx >= j_idx
    ident = jnp.where(t_idx == j_idx, 1.0, 0.0)
    tr = lax.broadcasted_iota(jnp.int32, (CHUNK, CHUNK), 0)
    tc = lax.broadcasted_iota(jnp.int32, (CHUNK, CHUNK), 1)
    ltri = jnp.where(tr >= tc, 1.0, 0.0).astype(BF16)
    nlev = int(math.log2(CHUNK)) - 1

    def expand(xc):
        xb = xc.astype(BF16)
        return jnp.where(bd16, jnp.concatenate([xb] * reps, axis=0), jnp.zeros((), BF16))

    def stack(a, b):
        return jnp.concatenate([a, b], axis=0).astype(BF16)

    def chunk_body(c, carry):
        rs = pl.ds(pl.multiple_of(c * CHUNK, CHUNK), CHUNK)
        ns = len(streams)
        vg, ar, kb, vexp, kbe, p_end = [], [], [], [], [], []
        for bi, g in streams:
            ls = slice(g * GROUP, (g + 1) * GROUP)
            lw = lw_ref[bi, rs, ls]
            hi, lo = _split2(lw)
            cum = _dot(ltri, hi) + _dot(ltri, lo)
            tail = cum[CHUNK - 1:CHUNK, :]
            e_neg = jnp.exp(-cum)
            e_tail = jnp.exp(tail - cum)
            k_ = k_ref[bi, rs, ls].astype(F32)
            b_ = b_ref[bi, rs, ls].astype(F32)
            v_ = v_ref[bi, rs, ls].astype(F32)
            vg.append(v_)
            ar.append(stack(an_ref[bi, rs, ls].astype(F32) * jnp.exp(cum - lw),
                            r_ref[bi, rs, ls].astype(F32) * jnp.exp(cum)))
            kb.append(jnp.concatenate([expand(k_ * e_neg), expand(b_ * e_neg)], axis=0))
            vexp.append(expand(v_))
            kbe.append(stack(k_ * e_tail, b_ * e_tail))
            p_end.append(jnp.exp(tail))

        amat = [_dot_nt(ar[i], kb[i]) for i in range(ns)]
        a_kc = [stack(jnp.where(tj_strict, amat[i][:CHUNK, :GROUP], 0.0),
                      jnp.where(tj_incl, amat[i][CHUNK:, :GROUP], 0.0)) for i in range(ns)]
        a_rb = [jnp.where(tj_incl, amat[i][CHUNK:, GROUP:], 0.0).astype(BF16) for i in range(ns)]
        xp = [jnp.where(tj_strict, amat[i][:CHUNK, GROUP:], 0.0) for i in range(ns)]

        t_inv = [ident + xp[i] for i in range(ns)]
        xp = [_dot(xp[i].astype(BF16), expand(xp[i])) for i in range(ns)]
        for lev in range(nlev):
            if lev < nlev - 1:
                both = [_dot(stack(xp[i], t_inv[i]), expand(xp[i])) for i in range(ns)]
                xp = [both[i][:CHUNK] for i in range(ns)]
                t_inv = [t_inv[i] + both[i][CHUNK:] for i in range(ns)]
            else:
                t_inv = [t_inv[i] + _dot(t_inv[i].astype(BF16), expand(xp[i])) for i in range(ns)]

        s_old = [s_ref[i] for i in range(ns)]
        ars = [_dot_nt(ar[i], s_old[i].astype(BF16)) for i in range(ns)]
        akv = [_dot(a_kc[i], vexp[i]) for i in range(ns)]
        z = [ars[i][:CHUNK] + akv[i][:CHUNK] for i in range(ns)]
        u = [_dot(t_inv[i].astype(BF16), expand(z[i])) for i in range(ns)]
        y = [ars[i][CHUNK:] + akv[i][CHUNK:] + _dot(a_rb[i], expand(u[i])) for i in range(ns)]
        ds = [_dot_tn(stack(vg[i], u[i]), kbe[i]) for i in range(ns)]
        for i, (bi, g) in enumerate(streams):
            y_ref[bi, rs, g * GROUP:(g + 1) * GROUP] = y[i]
            s_ref[i] = s_old[i] * p_end[i] + jnp.where(bd, ds[i], 0.0)
        return carry

    lax.fori_loop(0, tb // CHUNK, chunk_body, 0, unroll=True)


def _wkv(r, k, v, an, b, lw, tb):
    nb, t, rw = r.shape
    spec = pl.BlockSpec((nb, tb, rw), lambda i: (0, i, 0))
    return pl.pallas_call(
        _wkv_kernel,
        grid=(t // tb,),
        in_specs=[spec] * 6,
        out_specs=spec,
        out_shape=jax.ShapeDtypeStruct((nb, t, rw), F32),
        scratch_shapes=[pltpu.VMEM((nb * (rw // GROUP), GROUP, GROUP), F32)],
        compiler_params=_cparams(("arbitrary",)),
        name="wkv",
    )(r, k, v, an, b, lw)


def _outproj_kernel(oc_ref, y_ref, bonus_ref, g_ref, lg_ref, lb_ref, wa_ref, wb_ref, x_ref, g1_ref,
                    o_ref, *, tn):
    nchunks = o_ref.shape[1] // tn
    oc = oc_ref[...]
    acc = [_dot(oc, wa_ref[:, j * tn:(j + 1) * tn]) for j in range(nchunks)]
    ones_bd = _block_diag_ones()
    y = y_ref[...]
    mean = _head_sum(y, ones_bd) * (1.0 / HEAD_DIM)
    d = y - mean
    var = _head_sum(d * d, ones_bd) * (1.0 / HEAD_DIM)
    yn = d * lax.rsqrt(var + RWKV_GN_EPS) * lg_ref[...] + lb_ref[...]
    orw = ((yn + bonus_ref[...].astype(F32)) * g_ref[...].astype(F32)).astype(BF16)
    for j in range(nchunks):
        sl = slice(j * tn, (j + 1) * tn)
        o_ref[:, sl] = x_ref[:, sl] + g1_ref[:, sl] * (acc[j] + _dot(orw, wb_ref[:, sl]))


def _outproj(oc, y, bonus, g, lnx_g, lnx_b, w_out2, x, mod4, tm, tn):
    nb, t, d = x.shape
    cw, rw = oc.shape[2], y.shape[2]
    rows = lambda w: pl.BlockSpec((None, tm, w), lambda b, i: (b, i, 0))
    vec = pl.BlockSpec((1, rw), lambda b, i: (0, 0))
    return pl.pallas_call(
        functools.partial(_outproj_kernel, tn=tn),
        grid=(nb, t // tm),
        in_specs=[rows(cw), rows(rw), rows(rw), rows(rw), vec, vec,
                  pl.BlockSpec((None, cw, d), lambda b, i: (0, 0, 0), pipeline_mode=pl.Buffered(1)),
                  pl.BlockSpec((None, rw, d), lambda b, i: (1, 0, 0), pipeline_mode=pl.Buffered(1)),
                  rows(d),
                  pl.BlockSpec((None, None, 1, d), lambda b, i: (b, 2, 0, 0))],
        out_specs=rows(d),
        out_shape=jax.ShapeDtypeStruct((nb, t, d), F32),
        compiler_params=_cparams(("arbitrary", "arbitrary")),
        name="outproj",
    )(oc, y, bonus, g, lnx_g.reshape(1, rw), lnx_b.reshape(1, rw), w_out2, w_out2, x, mod4)


def _ffn_kernel(x_ref, g_ref, sc_ref, sh_ref, gate_ref, gf_ref, w1_ref, w3_ref, w2_ref,
                o_ref, h_ref, acc_ref, *, final_norm):
    j = pl.program_id(2)

    @pl.when(j == 0)
    def _():
        h_ref[...] = _modulated_norm(x_ref[...], g_ref[...], sc_ref[...], sh_ref[...]).astype(BF16)
        acc_ref[...] = jnp.zeros_like(acc_ref)

    h = h_ref[...]
    a = _dot(h, w1_ref[...])
    b = _dot(h, w3_ref[...])
    act = (a * _sigmoid(a) * b).astype(BF16)
    acc_ref[...] += _dot(act, w2_ref[...])

    @pl.when(j == pl.num_programs(2) - 1)
    def _():
        xo = x_ref[...] + gate_ref[...] * acc_ref[...]
        if final_norm:
            ms = jnp.mean(xo * xo, axis=-1, keepdims=True)
            xo = xo * lax.rsqrt(ms + RMS_EPS) * gf_ref[...]
        o_ref[...] = xo


def _ffn(x1, g, mod4, gf, w1, w3, w2, tm, tf, final_norm):
    nb, t, d = x1.shape
    f = w1.shape[1]
    modspec = lambda comp: pl.BlockSpec((None, None, 1, d), lambda b, i, j: (b, comp, 0, 0))
    vec = pl.BlockSpec((1, d), lambda b, i, j: (0, 0))
    return pl.pallas_call(
        functools.partial(_ffn_kernel, final_norm=final_norm),
        grid=(nb, t // tm, f // tf),
        in_specs=[pl.BlockSpec((None, tm, d), lambda b, i, j: (b, i, 0)),
                  vec, modspec(4), modspec(3), modspec(5), vec,
                  pl.BlockSpec((d, tf), lambda b, i, j: (0, j)),
                  pl.BlockSpec((d, tf), lambda b, i, j: (0, j)),
                  pl.BlockSpec((tf, d), lambda b, i, j: (j, 0))],
        out_specs=pl.BlockSpec((None, tm, d), lambda b, i, j: (b, i, 0)),
        out_shape=jax.ShapeDtypeStruct((nb, t, d), F32),
        scratch_shapes=[pltpu.VMEM((tm, d), BF16), pltpu.VMEM((tm, d), F32)],
        compiler_params=_cparams(("arbitrary", "arbitrary", "arbitrary")),
        name="ffn",
    )(x1, g.reshape(1, d), mod4, mod4, mod4, gf.reshape(1, d),
      w1.astype(BF16), w3.astype(BF16), w2.astype(BF16))


def _pad_cols(w, n):
    return jnp.concatenate([w, jnp.zeros(w.shape[:-1] + (n - w.shape[-1],), w.dtype)], axis=-1)


def _tile(n, pref):
    return pref if n % pref == 0 else n


def kernel(x, c, w_ada, b_ada, norm1_g, w_in, tshift_mu, conv_w, conv_b, conv_gn_g, conv_gn_b,
           w0, w_w2, a0, a_w2, g_w2, k_k, k_a, r_k, lnx_g, lnx_b, w_out, norm2_g,
           w_ff1, w_ff3, w_ff2, norm_f_g):
    nb, t, d = x.shape
    depth = w_in.shape[0]
    cw = conv_w.shape[2]
    rw = w0.shape[1]
    n_rest = w_in.shape[2] - 2 * cw
    n_pad = -(-n_rest // LANES) * LANES
    assert rw % GROUP == 0 and cw % GROUP == 0

    for l in range(depth):
        mod = _ada(c, w_ada[l], b_ada[l])
        mod4 = mod.reshape(nb, N_MOD, 1, d)
        oc, p = _front(x, norm1_g[l], mod4, w_in[l].astype(BF16), conv_w[l], conv_b[l],
                       conv_gn_g[l], conv_gn_b[l], cw, n_pad, _tile(t, 256), 512)
        r, k, v, an, b, lw, g, bonus = _prep(p, tshift_mu[l], w0[l], w_w2[l], a0[l], a_w2[l], g_w2[l],
                                             k_k[l], k_a[l], r_k[l].reshape(-1), rw, _tile(t, 256))
        y = _wkv(r, k, v, an, b, lw, _tile(t, 256))
        x = _outproj(oc, y, bonus, g, lnx_g[l], lnx_b[l], w_out[l].astype(BF16).reshape(2, cw, d),
                     x, mod4, _tile(t, 512), 512)
        x = _ffn(x, norm2_g[l], mod4, norm_f_g, w_ff1[l], w_ff3[l], w_ff2[l],
                 _tile(t, 512), 512, final_norm=(l == depth - 1))
    return x
```

```python
import functools
import math

import jax
import jax.numpy as jnp
from jax import lax
from jax.experimental import pallas as pl
from jax.experimental.pallas import tpu as pltpu

F32 = jnp.float32
BF16 = jnp.bfloat16

HEAD_DIM = 64
CONV_KERNEL = 31
RMS_EPS = 1e-6
CONV_GN_EPS = 1e-5
RWKV_GN_EPS = 64e-5
N_MOD = 6

LANES = 128
SUBLANES = 8
BF16_ROWS = 16
MXU_DIM = 256
GROUP = MXU_DIM
CHUNK = 64
HALO = 32
VMEM_LIMIT = 52 * 1024 * 1024


def _cparams(sem):
    return pltpu.CompilerParams(dimension_semantics=sem, vmem_limit_bytes=VMEM_LIMIT)


def _dot(a, b):
    return jnp.dot(a, b, preferred_element_type=F32)


def _dot_nt(a, b):
    return lax.dot_general(a, b, (((1,), (1,)), ((), ())), preferred_element_type=F32)


def _dot_tn(a, b):
    return lax.dot_general(a, b, (((0,), (0,)), ((), ())), preferred_element_type=F32)


def _split2(x):
    hi = x.astype(BF16)
    lo = (x - hi.astype(F32)).astype(BF16)
    return hi, lo


def _sigmoid(x):
    return 1.0 / (1.0 + jnp.exp(-x))


def _head_sum(x, ones_bd):
    outs = [_dot(x[:, g * GROUP:(g + 1) * GROUP].astype(BF16), ones_bd)
            for g in range(x.shape[1] // GROUP)]
    return outs[0] if len(outs) == 1 else jnp.concatenate(outs, axis=1)


def _block_diag_ones():
    r = lax.broadcasted_iota(jnp.int32, (GROUP, GROUP), 0) // HEAD_DIM
    c = lax.broadcasted_iota(jnp.int32, (GROUP, GROUP), 1) // HEAD_DIM
    return jnp.where(r == c, 1.0, 0.0).astype(BF16)


def _ada_kernel(cb_ref, w_ref, b_ref, wi_ref, o_ref, wib_ref):
    wib_ref[...] = wi_ref[...].astype(wib_ref.dtype)
    nb = cb_ref.shape[0]
    tn = w_ref.shape[1]
    for b in range(nb):
        cb = cb_ref[b]
        sc = cb * _sigmoid(cb)
        for j in range(tn // LANES):
            sl = slice(j * LANES, (j + 1) * LANES)
            s = jnp.sum(w_ref[:, sl] * sc, axis=0, keepdims=True)
            o_ref[b:b + 1, sl] = s + b_ref[:, sl]


def _ada(c, w_ada, b_ada, w_in):
    nb, d = c.shape
    n = w_ada.shape[1]
    tn = 768
    steps = n // tn
    rows = w_in.shape[0] // steps
    assert n % tn == 0 and rows * steps == w_in.shape[0] and rows % BF16_ROWS == 0
    cb = jnp.broadcast_to(c[:, :, None], (nb, d, LANES))
    wi_spec = pl.BlockSpec((rows, w_in.shape[1]), lambda j: (j, 0))
    return pl.pallas_call(
        _ada_kernel,
        grid=(steps,),
        in_specs=[pl.BlockSpec((nb, d, LANES), lambda j: (0, 0, 0)),
                  pl.BlockSpec((d, tn), lambda j: (0, j)),
                  pl.BlockSpec((1, tn), lambda j: (0, j)),
                  wi_spec],
        out_specs=[pl.BlockSpec((nb, tn), lambda j: (0, j)), wi_spec],
        out_shape=[jax.ShapeDtypeStruct((nb, n), F32), jax.ShapeDtypeStruct(w_in.shape, BF16)],
        compiler_params=_cparams(("arbitrary",)),
        name="ada",
    )(cb, w_ada, b_ada.reshape(1, n), w_in)


def _modulated_norm(x, g, sc, sh):
    ms = jnp.mean(x * x, axis=-1, keepdims=True)
    return (x * lax.rsqrt(ms + RMS_EPS) * g) * (1.0 + sc) + sh


def _front_kernel(x_ref, g_ref, sc_ref, sh_ref, w_ref, cw_ref, cb_ref, gg_ref, gb_ref,
                  wf1_ref, wf3_ref, wf2_ref, wo_ref,
                  oc_ref, p_ref, wf1b_ref, wf3b_ref, wf2b_ref, wob_ref,
                  ush_ref, acc_ref, carry_ref, *, tn):
    tm = x_ref.shape[0]
    cw = oc_ref.shape[1]
    nsh = ush_ref.shape[1]
    n_rest = w_ref.shape[1] - 2 * cw
    first = pl.program_id(1) == 0
    h = _modulated_norm(x_ref[...], g_ref[...], sc_ref[...], sh_ref[...]).astype(BF16)

    @pl.when(first)
    def _():
        carry_ref[...] = jnp.zeros_like(carry_ref)

    ush_ref[0, 0:HALO, :] = carry_ref[...]
    for j in range(cw // tn):
        sl = slice(j * tn, (j + 1) * tn)
        val = _dot(h, w_ref[:, sl])
        gate = _dot(h, w_ref[:, cw + j * tn:cw + (j + 1) * tn])
        ush_ref[0, HALO:HALO + tm, sl] = val * _sigmoid(gate)
    for src, dst in ((wf1_ref, wf1b_ref), (wf3_ref, wf3b_ref), (wf2_ref, wf2b_ref), (wo_ref, wob_ref)):
        dst[...] = src[...].astype(dst.dtype)
    carry_ref[...] = ush_ref[0, tm:tm + HALO, :]
    for phi in range(1, SUBLANES):
        ush_ref[phi, 0:nsh - SUBLANES, :] = ush_ref[0, phi:phi + nsh - SUBLANES, :]

    for lo in range(0, n_rest, tn):
        hi = min(lo + tn, n_rest)
        p_ref[:, lo:hi] = _dot(h, w_ref[:, 2 * cw + lo:2 * cw + hi]).astype(p_ref.dtype)
    if p_ref.shape[1] > n_rest:
        p_ref[:, n_rest:] = jnp.zeros((tm, p_ref.shape[1] - n_rest), p_ref.dtype)

    rb = 64
    base = HALO - (CONV_KERNEL - 1)
    for c in range(cw // LANES):
        sl = slice(c * LANES, (c + 1) * LANES)
        for r in range(tm // rb):
            acc = jnp.zeros((rb, LANES), F32) + cb_ref[:, sl]
            for j in range(CONV_KERNEL):
                q, phi = divmod(base + j, SUBLANES)
                lo = r * rb + q * SUBLANES
                acc = acc + cw_ref[j:j + 1, sl] * ush_ref[phi, lo:lo + rb, sl]
            acc_ref[r * rb:(r + 1) * rb, sl] = acc

    ones_bd = _block_diag_ones()
    u = acc_ref[...]
    mean = _head_sum(u, ones_bd) * (1.0 / HEAD_DIM)
    d = u - mean
    var = _head_sum(d * d, ones_bd) * (1.0 / HEAD_DIM)
    y = d * lax.rsqrt(var + CONV_GN_EPS) * gg_ref[...] + gb_ref[...]
    oc_ref[...] = (y * _sigmoid(y)).astype(oc_ref.dtype)


def _front(x, g, mod4, w, conv_w, conv_b, gn_g, gn_b, cw, n_pad, tm, tn, cast_weights):
    nb, t, d = x.shape
    n = w.shape[1]
    nt = t // tm
    row = lambda v: v.reshape(1, cw)
    vec = pl.BlockSpec((1, cw), lambda b, i: (0, 0))
    cast_specs = []
    for wt, revisits in cast_weights:
        rows = wt.shape[0] * revisits // (nb * nt)
        assert rows * nb * nt == wt.shape[0] * revisits and rows % BF16_ROWS == 0
        cast_specs.append(pl.BlockSpec((rows, wt.shape[1]),
                                       lambda b, i, rv=revisits: ((b * nt + i) // rv, 0)))
    return pl.pallas_call(
        functools.partial(_front_kernel, tn=tn),
        grid=(nb, t // tm),
        in_specs=[pl.BlockSpec((None, tm, d), lambda b, i: (b, i, 0)),
                  pl.BlockSpec((1, d), lambda b, i: (0, 0)),
                  pl.BlockSpec((None, None, 1, d), lambda b, i: (b, 1, 0, 0)),
                  pl.BlockSpec((None, None, 1, d), lambda b, i: (b, 0, 0, 0)),
                  pl.BlockSpec((d, n), lambda b, i: (0, 0), pipeline_mode=pl.Buffered(1)),
                  pl.BlockSpec((CONV_KERNEL, cw), lambda b, i: (0, 0)), vec, vec, vec] + cast_specs,
        out_specs=[pl.BlockSpec((None, tm, cw), lambda b, i: (b, i, 0)),
                   pl.BlockSpec((None, tm, n_pad), lambda b, i: (b, i, 0))] + cast_specs,
        out_shape=[jax.ShapeDtypeStruct((nb, t, cw), BF16),
                   jax.ShapeDtypeStruct((nb, t, n_pad), BF16)]
        + [jax.ShapeDtypeStruct(wt.shape, BF16) for wt, _ in cast_weights],
        scratch_shapes=[pltpu.VMEM((SUBLANES, tm + HALO, cw), F32), pltpu.VMEM((tm, cw), F32),
                        pltpu.VMEM((HALO, cw), F32)],
        compiler_params=_cparams(("arbitrary", "arbitrary")),
        name="front",
    )(x, g.reshape(1, d), mod4, mod4, w, conv_w, row(conv_b), row(gn_g), row(gn_b),
      *[wt for wt, _ in cast_weights])


def _prep_kernel(pr_ref, pk_ref, pv_ref, pl_ref, hr_ref, hk_ref, hv_ref, hl_ref,
                 mu_rkv_ref, mu_lora_ref,
                 w0_ref, ww2_ref, a0_ref, aw2_ref, gw2_ref, kk_ref, ka_ref, rk_ref,
                 r_ref, k_ref, v_ref, an_ref, b_ref, lw_ref, g_ref, bonus_ref, *, win_w, win_a, win_g):
    tm, rw = r_ref.shape
    first = pl.program_id(1) == 0
    rows = lax.broadcasted_iota(jnp.int32, (tm, 1), 0)

    def shifted(cur_ref, halo_ref, mu):
        cur = cur_ref[...].astype(F32)
        last = jnp.where(first, 0.0, halo_ref[BF16_ROWS - 1:BF16_ROWS, :].astype(F32))
        prev = jnp.where(rows == 0, last, pltpu.roll(cur, 1, axis=0))
        return cur + (prev - cur) * mu

    r = shifted(pr_ref, hr_ref, mu_rkv_ref[:, :rw])
    k = shifted(pk_ref, hk_ref, mu_rkv_ref[:, rw:2 * rw])
    v = shifted(pv_ref, hv_ref, mu_rkv_ref[:, 2 * rw:])
    ql = shifted(pl_ref, hl_ref, mu_lora_ref[...])
    xw, xa, xg = (ql[:, lo:hi] for lo, hi in (win_w, win_a, win_g))

    z = w0_ref[...] + _dot(jnp.tanh(xw).astype(BF16), ww2_ref[...])
    lw_ref[...] = (-math.exp(-0.5)) * _sigmoid(z)
    a = _sigmoid(a0_ref[...] + _dot(xa.astype(BF16), aw2_ref[...]))
    g_ref[...] = _dot(_sigmoid(xg).astype(BF16), gw2_ref[...]).astype(g_ref.dtype)

    ones_bd = _block_diag_ones()
    kk = k * kk_ref[...]
    ss = _head_sum(kk * kk, ones_bd)
    kkn = kk * lax.rsqrt(jnp.maximum(ss, 1e-24))
    kmod = k * (1.0 + (a - 1.0) * ka_ref[...])
    r_ref[...] = r.astype(r_ref.dtype)
    k_ref[...] = kmod.astype(k_ref.dtype)
    v_ref[...] = v.astype(v_ref.dtype)
    an_ref[...] = (-kkn).astype(an_ref.dtype)
    b_ref[...] = (kkn * a).astype(b_ref.dtype)
    bonus_ref[...] = (_head_sum(r * kmod * rk_ref[...], ones_bd) * v).astype(bonus_ref.dtype)


def _lane_window(lo, hi):
    return lo // LANES * LANES, -(-hi // LANES) * LANES


def _window_weight(w, lo, hi):
    a0, a1 = _lane_window(lo, hi)
    out = jnp.zeros((a1 - a0, w.shape[1]), w.dtype)
    return out.at[lo - a0:hi - a0].set(w)


def _prep(p, mu, w0, w_w2, a0, a_w2, g_w2, k_k, k_a, r_k, rw, tm):
    nb, t, n_pad = p.shape
    nrkv = 3 * rw
    nl = n_pad - nrkv
    dl, al, gl = w_w2.shape[0], a_w2.shape[0], g_w2.shape[0]
    assert nrkv % nl == 0
    r_blk = 0
    lora_blk = nrkv // nl
    wins = (_lane_window(0, dl), _lane_window(dl, dl + al), _lane_window(dl + al, dl + al + gl))
    ww2 = _window_weight(w_w2, 0, dl).astype(BF16)
    aw2 = _window_weight(a_w2, dl, dl + al).astype(BF16)
    gw2 = _window_weight(g_w2, dl + al, dl + al + gl).astype(BF16)
    mu_rkv = mu[:nrkv]
    mu_lora = _pad_cols(mu[nrkv:], nl)
    hb = tm // BF16_ROWS
    row = lambda v_: v_.reshape(1, -1)
    full = lambda shape: pl.BlockSpec(shape, lambda b, i: (0,) * len(shape))
    cur = lambda w, blk: pl.BlockSpec((None, tm, w), lambda b, i: (b, i, blk))
    halo = lambda w, blk: pl.BlockSpec((None, BF16_ROWS, w),
                                       lambda b, i: (b, jnp.maximum(i * hb - 1, 0), blk))
    out_spec = pl.BlockSpec((None, tm, rw), lambda b, i: (b, i, 0))
    bf = jax.ShapeDtypeStruct((nb, t, rw), BF16)
    return pl.pallas_call(
        functools.partial(_prep_kernel, win_w=wins[0], win_a=wins[1], win_g=wins[2]),
        grid=(nb, t // tm),
        in_specs=[cur(rw, r_blk), cur(rw, r_blk + 1), cur(rw, r_blk + 2), cur(nl, lora_blk),
                  halo(rw, r_blk), halo(rw, r_blk + 1), halo(rw, r_blk + 2), halo(nl, lora_blk),
                  full((1, nrkv)), full((1, nl)),
                  full((1, rw)), full(ww2.shape), full((1, rw)), full(aw2.shape),
                  full(gw2.shape), full((1, rw)), full((1, rw)), full((1, rw))],
        out_specs=[out_spec] * 8,
        out_shape=[bf, bf, bf, bf, bf, jax.ShapeDtypeStruct((nb, t, rw), F32), bf, bf],
        compiler_params=_cparams(("arbitrary", "arbitrary")),
        name="prep",
    )(p, p, p, p, p, p, p, p, row(mu_rkv), row(mu_lora), row(w0), ww2, row(a0), aw2, gw2,
      row(k_k), row(k_a), row(r_k))


def _wkv_kernel(r_ref, k_ref, v_ref, an_ref, b_ref, lw_ref, y_ref, s_ref):
    nb, tb, rw = r_ref.shape
    ngroups = rw // GROUP
    reps = GROUP // CHUNK
    streams = [(bi, g) for bi in range(nb) for g in range(ngroups)]

    @pl.when(pl.program_id(0) == 0)
    def _():
        s_ref[...] = jnp.zeros_like(s_ref)

    row = lax.broadcasted_iota(jnp.int32, (GROUP, GROUP), 0)
    col = lax.broadcasted_iota(jnp.int32, (GROUP, GROUP), 1)
    bd = (row // HEAD_DIM) == (col // HEAD_DIM)
    bd16 = (row // HEAD_DIM).astype(F32).astype(BF16) == (col // HEAD_DIM).astype(F32).astype(BF16)
    t_idx = lax.broadcasted_iota(jnp.int32, (CHUNK, GROUP), 0)
    j_idx = lax.broadcasted_iota(jnp.int32, (CHUNK, GROUP), 1) % CHUNK
    tj_strict = t_idx > j_idx
    tj_incl = t_idx >= j_idx
    ident = jnp.where(t_idx == j_idx, 1.0, 0.0)
    tr = lax.broadcasted_iota(jnp.int32, (CHUNK, CHUNK), 0)
    tc = lax.broadcasted_iota(jnp.int32, (CHUNK, CHUNK), 1)
    ltri = jnp.where(tr >= tc, 1.0, 0.0).astype(BF16)
    nlev = int(math.log2(CHUNK)) - 1

    def expand(xc):
        xb = xc.astype(BF16)
        return jnp.where(bd16, jnp.concatenate([xb] * reps, axis=0), jnp.zeros((), BF16))

    def stack(a, b):
        return jnp.concatenate([a, b], axis=0).astype(BF16)

    def chunk_body(c, carry):
        rs = pl.ds(pl.multiple_of(c * CHUNK, CHUNK), CHUNK)
        ns = len(streams)
        vg, ar, kb, vexp, kbe, p_end = [], [], [], [], [], []
        for bi, g in streams:
            ls = slice(g * GROUP, (g + 1) * GROUP)
            lw = lw_ref[bi, rs, ls]
            hi, lo = _split2(lw)
            cum = _dot(ltri, hi) + _dot(ltri, lo)
            tail = cum[CHUNK - 1:CHUNK, :]
            e_neg = jnp.exp(-cum)
            e_tail = jnp.exp(tail - cum)
            k_ = k_ref[bi, rs, ls].astype(F32)
            b_ = b_ref[bi, rs, ls].astype(F32)
            v_ = v_ref[bi, rs, ls].astype(F32)
            vg.append(v_)
            ar.append(stack(an_ref[bi, rs, ls].astype(F32) * jnp.exp(cum - lw),
                            r_ref[bi, rs, ls].astype(F32) * jnp.exp(cum)))
            kb.append(jnp.concatenate([expand(k_ * e_neg), expand(b_ * e_neg)], axis=0))
            vexp.append(expand(v_))
            kbe.append(stack(k_ * e_tail, b_ * e_tail))
            p_end.append(jnp.exp(tail))

        amat = [_dot_nt(ar[i], kb[i]) for i in range(ns)]
        a_kc = [stack(jnp.where(tj_strict, amat[i][:CHUNK, :GROUP], 0.0),
                      jnp.where(tj_incl, amat[i][CHUNK:, :GROUP], 0.0)) for i in range(ns)]
        a_rb = [jnp.where(tj_incl, amat[i][CHUNK:, GROUP:], 0.0).astype(BF16) for i in range(ns)]
        xp = [jnp.where(tj_strict, amat[i][:CHUNK, GROUP:], 0.0) for i in range(ns)]

        t_inv = [ident + xp[i] for i in range(ns)]
        xp = [_dot(xp[i].astype(BF16), expand(xp[i])) for i in range(ns)]
        for lev in range(nlev):
            if lev < nlev - 1:
                both = [_dot(stack(xp[i], t_inv[i]), expand(xp[i])) for i in range(ns)]
                xp = [both[i][:CHUNK] for i in range(ns)]
                t_inv = [t_inv[i] + both[i][CHUNK:] for i in range(ns)]
            else:
                t_inv = [t_inv[i] + _dot(t_inv[i].astype(BF16), expand(xp[i])) for i in range(ns)]

        s_old = [s_ref[i] for i in range(ns)]
        ars = [_dot_nt(ar[i], s_old[i].astype(BF16)) for i in range(ns)]
        akv = [_dot(a_kc[i], vexp[i]) for i in range(ns)]
        z = [ars[i][:CHUNK] + akv[i][:CHUNK] for i in range(ns)]
        u = [_dot(t_inv[i].astype(BF16), expand(z[i])) for i in range(ns)]
        y = [ars[i][CHUNK:] + akv[i][CHUNK:] + _dot(a_rb[i], expand(u[i])) for i in range(ns)]
        ds = [_dot_tn(stack(vg[i], u[i]), kbe[i]) for i in range(ns)]
        for i, (bi, g) in enumerate(streams):
            y_ref[bi, rs, g * GROUP:(g + 1) * GROUP] = y[i]
            s_ref[i] = s_old[i] * p_end[i] + jnp.where(bd, ds[i], 0.0)
        return carry

    lax.fori_loop(0, tb // CHUNK, chunk_body, 0, unroll=True)


def _wkv(r, k, v, an, b, lw, tb):
    nb, t, rw = r.shape
    spec = pl.BlockSpec((nb, tb, rw), lambda i: (0, i, 0))
    return pl.pallas_call(
        _wkv_kernel,
        grid=(t // tb,),
        in_specs=[spec] * 6,
        out_specs=spec,
        out_shape=jax.ShapeDtypeStruct((nb, t, rw), F32),
        scratch_shapes=[pltpu.VMEM((nb * (rw // GROUP), GROUP, GROUP), F32)],
        compiler_params=_cparams(("arbitrary",)),
        name="wkv",
    )(r, k, v, an, b, lw)


def _outproj_kernel(oc_ref, y_ref, bonus_ref, g_ref, lg_ref, lb_ref, wa_ref, wb_ref, x_ref, g1_ref,
                    o_ref, *, tn):
    nchunks = o_ref.shape[1] // tn
    oc = oc_ref[...]
    acc = [_dot(oc, wa_ref[:, j * tn:(j + 1) * tn]) for j in range(nchunks)]
    ones_bd = _block_diag_ones()
    y = y_ref[...]
    mean = _head_sum(y, ones_bd) * (1.0 / HEAD_DIM)
    d = y - mean
    var = _head_sum(d * d, ones_bd) * (1.0 / HEAD_DIM)
    yn = d * lax.rsqrt(var + RWKV_GN_EPS) * lg_ref[...] + lb_ref[...]
    orw = ((yn + bonus_ref[...].astype(F32)) * g_ref[...].astype(F32)).astype(BF16)
    for j in range(nchunks):
        sl = slice(j * tn, (j + 1) * tn)
        o_ref[:, sl] = x_ref[:, sl] + g1_ref[:, sl] * (acc[j] + _dot(orw, wb_ref[:, sl]))


def _outproj(oc, y, bonus, g, lnx_g, lnx_b, w_out2, x, mod4, tm, tn):
    nb, t, d = x.shape
    cw, rw = oc.shape[2], y.shape[2]
    rows = lambda w: pl.BlockSpec((None, tm, w), lambda b, i: (b, i, 0))
    vec = pl.BlockSpec((1, rw), lambda b, i: (0, 0))
    return pl.pallas_call(
        functools.partial(_outproj_kernel, tn=tn),
        grid=(nb, t // tm),
        in_specs=[rows(cw), rows(rw), rows(rw), rows(rw), vec, vec,
                  pl.BlockSpec((None, cw, d), lambda b, i: (0, 0, 0), pipeline_mode=pl.Buffered(1)),
                  pl.BlockSpec((None, rw, d), lambda b, i: (1, 0, 0), pipeline_mode=pl.Buffered(1)),
                  rows(d),
                  pl.BlockSpec((None, None, 1, d), lambda b, i: (b, 2, 0, 0))],
        out_specs=rows(d),
        out_shape=jax.ShapeDtypeStruct((nb, t, d), F32),
        compiler_params=_cparams(("arbitrary", "arbitrary")),
        name="outproj",
    )(oc, y, bonus, g, lnx_g.reshape(1, rw), lnx_b.reshape(1, rw), w_out2, w_out2, x, mod4)


def _ffn_kernel(x_ref, g_ref, sc_ref, sh_ref, gate_ref, gf_ref, w1_ref, w3_ref, w2_ref,
                o_ref, h_ref, acc_ref, *, final_norm):
    j = pl.program_id(2)

    @pl.when(j == 0)
    def _():
        h_ref[...] = _modulated_norm(x_ref[...], g_ref[...], sc_ref[...], sh_ref[...]).astype(BF16)
        acc_ref[...] = jnp.zeros_like(acc_ref)

    h = h_ref[...]
    a = _dot(h, w1_ref[...])
    b = _dot(h, w3_ref[...])
    act = (a * _sigmoid(a) * b).astype(BF16)
    acc_ref[...] += _dot(act, w2_ref[...])

    @pl.when(j == pl.num_programs(2) - 1)
    def _():
        xo = x_ref[...] + gate_ref[...] * acc_ref[...]
        if final_norm:
            ms = jnp.mean(xo * xo, axis=-1, keepdims=True)
            xo = xo * lax.rsqrt(ms + RMS_EPS) * gf_ref[...]
        o_ref[...] = xo


def _ffn(x1, g, mod4, gf, w1, w3, w2, tm, tf, final_norm):
    nb, t, d = x1.shape
    f = w1.shape[1]
    modspec = lambda comp: pl.BlockSpec((None, None, 1, d), lambda b, i, j: (b, comp, 0, 0))
    vec = pl.BlockSpec((1, d), lambda b, i, j: (0, 0))
    return pl.pallas_call(
        functools.partial(_ffn_kernel, final_norm=final_norm),
        grid=(nb, t // tm, f // tf),
        in_specs=[pl.BlockSpec((None, tm, d), lambda b, i, j: (b, i, 0)),
                  vec, modspec(4), modspec(3), modspec(5), vec,
                  pl.BlockSpec((d, tf), lambda b, i, j: (0, j)),
                  pl.BlockSpec((d, tf), lambda b, i, j: (0, j)),
                  pl.BlockSpec((tf, d), lambda b, i, j: (j, 0))],
        out_specs=pl.BlockSpec((None, tm, d), lambda b, i, j: (b, i, 0)),
        out_shape=jax.ShapeDtypeStruct((nb, t, d), F32),
        scratch_shapes=[pltpu.VMEM((tm, d), BF16), pltpu.VMEM((tm, d), F32)],
        compiler_params=_cparams(("arbitrary", "arbitrary", "arbitrary")),
        name="ffn",
    )(x1, g.reshape(1, d), mod4, mod4, mod4, gf.reshape(1, d), w1, w3, w2)


def _pad_cols(w, n):
    return jnp.concatenate([w, jnp.zeros(w.shape[:-1] + (n - w.shape[-1],), w.dtype)], axis=-1)


def _tile(n, pref):
    return pref if n % pref == 0 else n


def kernel(x, c, w_ada, b_ada, norm1_g, w_in, tshift_mu, conv_w, conv_b, conv_gn_g, conv_gn_b,
           w0, w_w2, a0, a_w2, g_w2, k_k, k_a, r_k, lnx_g, lnx_b, w_out, norm2_g,
           w_ff1, w_ff3, w_ff2, norm_f_g):
    nb, t, d = x.shape
    depth = w_in.shape[0]
    cw = conv_w.shape[2]
    rw = w0.shape[1]
    n_rest = w_in.shape[2] - 2 * cw
    n_pad = -(-n_rest // LANES) * LANES
    assert rw % GROUP == 0 and cw % GROUP == 0

    for l in range(depth):
        mod, wib = _ada(c, w_ada[l], b_ada[l], w_in[l])
        mod4 = mod.reshape(nb, N_MOD, 1, d)
        oc, p, w1b, w3b, w2b, wob = _front(
            x, norm1_g[l], mod4, wib, conv_w[l], conv_b[l], conv_gn_g[l], conv_gn_b[l],
            cw, n_pad, _tile(t, 256), 512,
            cast_weights=((w_ff1[l], 1), (w_ff3[l], 1), (w_ff2[l], 2), (w_out[l], 1)))
        r, k, v, an, b, lw, g, bonus = _prep(p, tshift_mu[l], w0[l], w_w2[l], a0[l], a_w2[l], g_w2[l],
                                             k_k[l], k_a[l], r_k[l].reshape(-1), rw, _tile(t, 256))
        y = _wkv(r, k, v, an, b, lw, _tile(t, 256))
        x = _outproj(oc, y, bonus, g, lnx_g[l], lnx_b[l], wob.reshape(2, cw, d),
                     x, mod4, _tile(t, 512), 512)
        x = _ffn(x, norm2_g[l], mod4, norm_f_g, w1b, w3b, w2b,
                 _tile(t, 512), 512, final_norm=(l == depth - 1))
    return x
```

```python
import functools
import math

import jax
import jax.numpy as jnp
from jax import lax
from jax.experimental import pallas as pl
from jax.experimental.pallas import tpu as pltpu

F32 = jnp.float32
BF16 = jnp.bfloat16

HEAD_DIM = 64
CONV_KERNEL = 31
RMS_EPS = 1e-6
CONV_GN_EPS = 1e-5
RWKV_GN_EPS = 64e-5
N_MOD = 6

LANES = 128
SUBLANES = 8
BF16_ROWS = 16
MXU_DIM = 256
GROUP = MXU_DIM
CHUNK = 64
HALO = 32
VMEM_LIMIT = 52 * 1024 * 1024


def _cparams(sem):
    return pltpu.CompilerParams(dimension_semantics=sem, vmem_limit_bytes=VMEM_LIMIT)


def _dot(a, b):
    return jnp.dot(a, b, preferred_element_type=F32)


def _dot_nt(a, b):
    return lax.dot_general(a, b, (((1,), (1,)), ((), ())), preferred_element_type=F32)


def _dot_tn(a, b):
    return lax.dot_general(a, b, (((0,), (0,)), ((), ())), preferred_element_type=F32)


def _split2(x):
    hi = x.astype(BF16)
    lo = (x - hi.astype(F32)).astype(BF16)
    return hi, lo


def _sigmoid(x):
    return 1.0 / (1.0 + jnp.exp(-x))


def _head_sum(x, ones_bd):
    outs = [_dot(x[:, g * GROUP:(g + 1) * GROUP].astype(BF16), ones_bd)
            for g in range(x.shape[1] // GROUP)]
    return outs[0] if len(outs) == 1 else jnp.concatenate(outs, axis=1)


def _block_diag_ones():
    r = lax.broadcasted_iota(jnp.int32, (GROUP, GROUP), 0) // HEAD_DIM
    c = lax.broadcasted_iota(jnp.int32, (GROUP, GROUP), 1) // HEAD_DIM
    return jnp.where(r == c, 1.0, 0.0).astype(BF16)


def _ada_kernel(cb_ref, w_ref, b_ref, wi_ref, o_ref, wib_ref):
    wib_ref[...] = wi_ref[...].astype(wib_ref.dtype)
    nb = cb_ref.shape[0]
    tn = w_ref.shape[1]
    for b in range(nb):
        cb = cb_ref[b]
        sc = cb * _sigmoid(cb)
        for j in range(tn // LANES):
            sl = slice(j * LANES, (j + 1) * LANES)
            s = jnp.sum(w_ref[:, sl] * sc, axis=0, keepdims=True)
            o_ref[b:b + 1, sl] = s + b_ref[:, sl]


def _ada(c, w_ada, b_ada, w_in_t):
    nb, d = c.shape
    n = w_ada.shape[1]
    tn = 1024
    steps = n // tn
    rows = w_in_t.shape[0] // steps
    assert n % tn == 0 and rows * steps == w_in_t.shape[0] and rows % BF16_ROWS == 0
    cb = jnp.broadcast_to(c[:, :, None], (nb, d, LANES))
    wi_spec = pl.BlockSpec((rows, w_in_t.shape[1]), lambda j: (j, 0))
    return pl.pallas_call(
        _ada_kernel,
        grid=(steps,),
        in_specs=[pl.BlockSpec((nb, d, LANES), lambda j: (0, 0, 0)),
                  pl.BlockSpec((d, tn), lambda j: (0, j)),
                  pl.BlockSpec((1, tn), lambda j: (0, j)),
                  wi_spec],
        out_specs=[pl.BlockSpec((nb, tn), lambda j: (0, j)), wi_spec],
        out_shape=[jax.ShapeDtypeStruct((nb, n), F32), jax.ShapeDtypeStruct(w_in_t.shape, BF16)],
        compiler_params=_cparams(("arbitrary",)),
        name="ada",
    )(cb, w_ada, b_ada.reshape(1, n), w_in_t)


def _modulated_norm(x, g, sc, sh):
    ms = jnp.mean(x * x, axis=-1, keepdims=True)
    return (x * lax.rsqrt(ms + RMS_EPS) * g) * (1.0 + sc) + sh


def _front_kernel(x_ref, g_ref, sc_ref, sh_ref, wt_ref, cw_ref, cb_ref, gg_ref, gb_ref,
                  wf1_ref, wf3_ref, wf2_ref, wo_ref,
                  oc_ref, p_ref, wf1b_ref, wf3b_ref, wf2b_ref, wob_ref,
                  ush_ref, acc_ref, carry_ref, *, tn):
    tm = x_ref.shape[0]
    cw = oc_ref.shape[1]
    nsh = ush_ref.shape[1]
    n_rest = wt_ref.shape[0] - 2 * cw
    first = pl.program_id(1) == 0
    h = _modulated_norm(x_ref[...], g_ref[...], sc_ref[...], sh_ref[...]).astype(BF16)

    @pl.when(first)
    def _():
        carry_ref[...] = jnp.zeros_like(carry_ref)

    ush_ref[0, 0:HALO, :] = carry_ref[...]
    for j in range(cw // tn):
        sl = slice(j * tn, (j + 1) * tn)
        val = _dot_nt(h, wt_ref[sl, :])
        gate = _dot_nt(h, wt_ref[cw + j * tn:cw + (j + 1) * tn, :])
        ush_ref[0, HALO:HALO + tm, sl] = val * _sigmoid(gate)
    for src, dst in ((wf1_ref, wf1b_ref), (wf3_ref, wf3b_ref), (wf2_ref, wf2b_ref), (wo_ref, wob_ref)):
        dst[...] = src[...].astype(dst.dtype)
    carry_ref[...] = ush_ref[0, tm:tm + HALO, :]
    for phi in range(1, SUBLANES):
        ush_ref[phi, 0:nsh - SUBLANES, :] = ush_ref[0, phi:phi + nsh - SUBLANES, :]

    for lo in range(0, n_rest, tn):
        hi = min(lo + tn, n_rest)
        p_ref[:, lo:hi] = _dot_nt(h, wt_ref[2 * cw + lo:2 * cw + hi, :]).astype(p_ref.dtype)
    if p_ref.shape[1] > n_rest:
        p_ref[:, n_rest:] = jnp.zeros((tm, p_ref.shape[1] - n_rest), p_ref.dtype)

    rb = 64
    base = HALO - (CONV_KERNEL - 1)
    for c in range(cw // LANES):
        sl = slice(c * LANES, (c + 1) * LANES)
        for r in range(tm // rb):
            acc = jnp.zeros((rb, LANES), F32) + cb_ref[:, sl]
            for j in range(CONV_KERNEL):
                q, phi = divmod(base + j, SUBLANES)
                lo = r * rb + q * SUBLANES
                acc = acc + cw_ref[j:j + 1, sl] * ush_ref[phi, lo:lo + rb, sl]
            acc_ref[r * rb:(r + 1) * rb, sl] = acc

    ones_bd = _block_diag_ones()
    u = acc_ref[...]
    mean = _head_sum(u, ones_bd) * (1.0 / HEAD_DIM)
    d = u - mean
    var = _head_sum(d * d, ones_bd) * (1.0 / HEAD_DIM)
    y = d * lax.rsqrt(var + CONV_GN_EPS) * gg_ref[...] + gb_ref[...]
    oc_ref[...] = (y * _sigmoid(y)).astype(oc_ref.dtype)


def _front(x, g, mod4, wt, conv_w, conv_b, gn_g, gn_b, cw, n_pad, tm, tn, cast_weights):
    nb, t, d = x.shape
    n = wt.shape[0]
    nt = t // tm
    row = lambda v: v.reshape(1, cw)
    vec = pl.BlockSpec((1, cw), lambda b, i: (0, 0))
    cast_specs = []
    for cwt, revisits in cast_weights:
        rows = cwt.shape[0] * revisits // (nb * nt)
        assert rows * nb * nt == cwt.shape[0] * revisits and rows % BF16_ROWS == 0
        cast_specs.append(pl.BlockSpec((rows, cwt.shape[1]),
                                       lambda b, i, rv=revisits: ((b * nt + i) // rv, 0)))
    return pl.pallas_call(
        functools.partial(_front_kernel, tn=tn),
        grid=(nb, t // tm),
        in_specs=[pl.BlockSpec((None, tm, d), lambda b, i: (b, i, 0)),
                  pl.BlockSpec((1, d), lambda b, i: (0, 0)),
                  pl.BlockSpec((None, None, 1, d), lambda b, i: (b, 1, 0, 0)),
                  pl.BlockSpec((None, None, 1, d), lambda b, i: (b, 0, 0, 0)),
                  pl.BlockSpec((n, d), lambda b, i: (0, 0), pipeline_mode=pl.Buffered(1)),
                  pl.BlockSpec((CONV_KERNEL, cw), lambda b, i: (0, 0)), vec, vec, vec] + cast_specs,
        out_specs=[pl.BlockSpec((None, tm, cw), lambda b, i: (b, i, 0)),
                   pl.BlockSpec((None, tm, n_pad), lambda b, i: (b, i, 0))] + cast_specs,
        out_shape=[jax.ShapeDtypeStruct((nb, t, cw), BF16),
                   jax.ShapeDtypeStruct((nb, t, n_pad), BF16)]
        + [jax.ShapeDtypeStruct(cwt.shape, BF16) for cwt, _ in cast_weights],
        scratch_shapes=[pltpu.VMEM((SUBLANES, tm + HALO, cw), F32), pltpu.VMEM((tm, cw), F32),
                        pltpu.VMEM((HALO, cw), F32)],
        compiler_params=_cparams(("arbitrary", "arbitrary")),
        name="front",
    )(x, g.reshape(1, d), mod4, mod4, wt, conv_w, row(conv_b), row(gn_g), row(gn_b),
      *[cwt for cwt, _ in cast_weights])


def _prep_kernel(pr_ref, pk_ref, pv_ref, pl_ref, hr_ref, hk_ref, hv_ref, hl_ref,
                 mu_rkv_ref, mu_lora_ref,
                 w0_ref, ww2_ref, a0_ref, aw2_ref, gw2_ref, kk_ref, ka_ref, rk_ref,
                 r_ref, k_ref, v_ref, an_ref, b_ref, lw_ref, g_ref, bonus_ref, *, win_w, win_a, win_g):
    tm, rw = r_ref.shape
    first = pl.program_id(1) == 0
    rows = lax.broadcasted_iota(jnp.int32, (tm, 1), 0)

    def shifted(cur_ref, halo_ref, mu):
        cur = cur_ref[...].astype(F32)
        last = jnp.where(first, 0.0, halo_ref[BF16_ROWS - 1:BF16_ROWS, :].astype(F32))
        prev = jnp.where(rows == 0, last, pltpu.roll(cur, 1, axis=0))
        return cur + (prev - cur) * mu

    r = shifted(pr_ref, hr_ref, mu_rkv_ref[:, :rw])
    k = shifted(pk_ref, hk_ref, mu_rkv_ref[:, rw:2 * rw])
    v = shifted(pv_ref, hv_ref, mu_rkv_ref[:, 2 * rw:])
    ql = shifted(pl_ref, hl_ref, mu_lora_ref[...])
    xw, xa, xg = (ql[:, lo:hi] for lo, hi in (win_w, win_a, win_g))

    z = w0_ref[...] + _dot(jnp.tanh(xw).astype(BF16), ww2_ref[...])
    lw_ref[...] = (-math.exp(-0.5)) * _sigmoid(z)
    a = _sigmoid(a0_ref[...] + _dot(xa.astype(BF16), aw2_ref[...]))
    g_ref[...] = _dot(_sigmoid(xg).astype(BF16), gw2_ref[...]).astype(g_ref.dtype)

    ones_bd = _block_diag_ones()
    kk = k * kk_ref[...]
    ss = _head_sum(kk * kk, ones_bd)
    kkn = kk * lax.rsqrt(jnp.maximum(ss, 1e-24))
    kmod = k * (1.0 + (a - 1.0) * ka_ref[...])
    r_ref[...] = r.astype(r_ref.dtype)
    k_ref[...] = kmod.astype(k_ref.dtype)
    v_ref[...] = v.astype(v_ref.dtype)
    an_ref[...] = (-kkn).astype(an_ref.dtype)
    b_ref[...] = (kkn * a).astype(b_ref.dtype)
    bonus_ref[...] = (_head_sum(r * kmod * rk_ref[...], ones_bd) * v).astype(bonus_ref.dtype)


def _lane_window(lo, hi):
    return lo // LANES * LANES, -(-hi // LANES) * LANES


def _window_weight(w, lo, hi):
    a0, a1 = _lane_window(lo, hi)
    out = jnp.zeros((a1 - a0, w.shape[1]), w.dtype)
    return out.at[lo - a0:hi - a0].set(w)


def _prep(p, mu, w0, w_w2, a0, a_w2, g_w2, k_k, k_a, r_k, rw, tm):
    nb, t, n_pad = p.shape
    nrkv = 3 * rw
    nl = n_pad - nrkv
    dl, al, gl = w_w2.shape[0], a_w2.shape[0], g_w2.shape[0]
    assert nrkv % nl == 0
    r_blk = 0
    lora_blk = nrkv // nl
    wins = (_lane_window(0, dl), _lane_window(dl, dl + al), _lane_window(dl + al, dl + al + gl))
    ww2 = _window_weight(w_w2, 0, dl).astype(BF16)
    aw2 = _window_weight(a_w2, dl, dl + al).astype(BF16)
    gw2 = _window_weight(g_w2, dl + al, dl + al + gl).astype(BF16)
    mu_rkv = mu[:nrkv]
    mu_lora = _pad_cols(mu[nrkv:], nl)
    hb = tm // BF16_ROWS
    row = lambda v_: v_.reshape(1, -1)
    full = lambda shape: pl.BlockSpec(shape, lambda b, i: (0,) * len(shape))
    cur = lambda w, blk: pl.BlockSpec((None, tm, w), lambda b, i: (b, i, blk))
    halo = lambda w, blk: pl.BlockSpec((None, BF16_ROWS, w),
                                       lambda b, i: (b, jnp.maximum(i * hb - 1, 0), blk))
    out_spec = pl.BlockSpec((None, tm, rw), lambda b, i: (b, i, 0))
    bf = jax.ShapeDtypeStruct((nb, t, rw), BF16)
    return pl.pallas_call(
        functools.partial(_prep_kernel, win_w=wins[0], win_a=wins[1], win_g=wins[2]),
        grid=(nb, t // tm),
        in_specs=[cur(rw, r_blk), cur(rw, r_blk + 1), cur(rw, r_blk + 2), cur(nl, lora_blk),
                  halo(rw, r_blk), halo(rw, r_blk + 1), halo(rw, r_blk + 2), halo(nl, lora_blk),
                  full((1, nrkv)), full((1, nl)),
                  full((1, rw)), full(ww2.shape), full((1, rw)), full(aw2.shape),
                  full(gw2.shape), full((1, rw)), full((1, rw)), full((1, rw))],
        out_specs=[out_spec] * 8,
        out_shape=[bf, bf, bf, bf, bf, jax.ShapeDtypeStruct((nb, t, rw), F32), bf, bf],
        compiler_params=_cparams(("arbitrary", "arbitrary")),
        name="prep",
    )(p, p, p, p, p, p, p, p, row(mu_rkv), row(mu_lora), row(w0), ww2, row(a0), aw2, gw2,
      row(k_k), row(k_a), row(r_k))


def _wkv_kernel(r_ref, k_ref, v_ref, an_ref, b_ref, lw_ref, y_ref, s_ref):
    nb, tb, rw = r_ref.shape
    ngroups = rw // GROUP
    reps = GROUP // CHUNK
    streams = [(bi, g) for bi in range(nb) for g in range(ngroups)]

    @pl.when(pl.program_id(0) == 0)
    def _():
        s_ref[...] = jnp.zeros_like(s_ref)

    row = lax.broadcasted_iota(jnp.int32, (GROUP, GROUP), 0)
    col = lax.broadcasted_iota(jnp.int32, (GROUP, GROUP), 1)
    bd = (row // HEAD_DIM) == (col // HEAD_DIM)
    bd16 = (row // HEAD_DIM).astype(F32).astype(BF16) == (col // HEAD_DIM).astype(F32).astype(BF16)
    t_idx = lax.broadcasted_iota(jnp.int32, (CHUNK, GROUP), 0)
    j_idx = lax.broadcasted_iota(jnp.int32, (CHUNK, GROUP), 1) % CHUNK
    tj_strict = t_idx > j_idx
    tj_incl = t_idx >= j_idx
    ident = jnp.where(t_idx == j_idx, 1.0, 0.0)
    tr = lax.broadcasted_iota(jnp.int32, (CHUNK, CHUNK), 0)
    tc = lax.broadcasted_iota(jnp.int32, (CHUNK, CHUNK), 1)
    ltri = jnp.where(tr >= tc, 1.0, 0.0).astype(BF16)
    nlev = int(math.log2(CHUNK)) - 1

    def expand(xc):
        xb = xc.astype(BF16)
        return jnp.where(bd16, jnp.concatenate([xb] * reps, axis=0), jnp.zeros((), BF16))

    def stack(a, b):
        return jnp.concatenate([a, b], axis=0).astype(BF16)

    def chunk_body(c, carry):
        rs = pl.ds(pl.multiple_of(c * CHUNK, CHUNK), CHUNK)
        ns = len(streams)
        vg, ar, kb, vexp, kbe, p_end = [], [], [], [], [], []
        for bi, g in streams:
            ls = slice(g * GROUP, (g + 1) * GROUP)
            lw = lw_ref[bi, rs, ls]
            hi, lo = _split2(lw)
            cum = _dot(ltri, hi) + _dot(ltri, lo)
            tail = cum[CHUNK - 1:CHUNK, :]
            e_neg = jnp.exp(-cum)
            e_tail = jnp.exp(tail - cum)
            k_ = k_ref[bi, rs, ls].astype(F32)
            b_ = b_ref[bi, rs, ls].astype(F32)
            v_ = v_ref[bi, rs, ls].astype(F32)
            vg.append(v_)
            ar.append(stack(an_ref[bi, rs, ls].astype(F32) * jnp.exp(cum - lw),
                            r_ref[bi, rs, ls].astype(F32) * jnp.exp(cum)))
            kb.append(jnp.concatenate([expand(k_ * e_neg), expand(b_ * e_neg)], axis=0))
            vexp.append(expand(v_))
            kbe.append(stack(k_ * e_tail, b_ * e_tail))
            p_end.append(jnp.exp(tail))

        amat = [_dot_nt(ar[i], kb[i]) for i in range(ns)]
        a_kc = [stack(jnp.where(tj_strict, amat[i][:CHUNK, :GROUP], 0.0),
                      jnp.where(tj_incl, amat[i][CHUNK:, :GROUP], 0.0)) for i in range(ns)]
        a_rb = [jnp.where(tj_incl, amat[i][CHUNK:, GROUP:], 0.0).astype(BF16) for i in range(ns)]
        xp = [jnp.where(tj_strict, amat[i][:CHUNK, GROUP:], 0.0) for i in range(ns)]

        t_inv = [ident + xp[i] for i in range(ns)]
        xp = [_dot(xp[i].astype(BF16), expand(xp[i])) for i in range(ns)]
        for lev in range(nlev):
            if lev < nlev - 1:
                both = [_dot(stack(xp[i], t_inv[i]), expand(xp[i])) for i in range(ns)]
                xp = [both[i][:CHUNK] for i in range(ns)]
                t_inv = [t_inv[i] + both[i][CHUNK:] for i in range(ns)]
            else:
                t_inv = [t_inv[i] + _dot(t_inv[i].astype(BF16), expand(xp[i])) for i in range(ns)]

        s_old = [s_ref[i] for i in range(ns)]
        ars = [_dot_nt(ar[i], s_old[i].astype(BF16)) for i in range(ns)]
        akv = [_dot(a_kc[i], vexp[i]) for i in range(ns)]
        z = [ars[i][:CHUNK] + akv[i][:CHUNK] for i in range(ns)]
        u = [_dot(t_inv[i].astype(BF16), expand(z[i])) for i in range(ns)]
        y = [ars[i][CHUNK:] + akv[i][CHUNK:] + _dot(a_rb[i], expand(u[i])) for i in range(ns)]
        ds = [_dot_tn(stack(vg[i], u[i]), kbe[i]) for i in range(ns)]
        for i, (bi, g) in enumerate(streams):
            y_ref[bi, rs, g * GROUP:(g + 1) * GROUP] = y[i]
            s_ref[i] = s_old[i] * p_end[i] + jnp.where(bd, ds[i], 0.0)
        return carry

    lax.fori_loop(0, tb // CHUNK, chunk_body, 0, unroll=True)


def _wkv(r, k, v, an, b, lw, tb):
    nb, t, rw = r.shape
    spec = pl.BlockSpec((nb, tb, rw), lambda i: (0, i, 0))
    return pl.pallas_call(
        _wkv_kernel,
        grid=(t // tb,),
        in_specs=[spec] * 6,
        out_specs=spec,
        out_shape=jax.ShapeDtypeStruct((nb, t, rw), F32),
        scratch_shapes=[pltpu.VMEM((nb * (rw // GROUP), GROUP, GROUP), F32)],
        compiler_params=_cparams(("arbitrary",)),
        name="wkv",
    )(r, k, v, an, b, lw)


def _outproj_kernel(oc_ref, y_ref, bonus_ref, g_ref, lg_ref, lb_ref, wa_ref, wb_ref, x_ref, g1_ref,
                    o_ref, *, tn):
    nchunks = o_ref.shape[1] // tn
    oc = oc_ref[...]
    acc = [_dot(oc, wa_ref[:, j * tn:(j + 1) * tn]) for j in range(nchunks)]
    ones_bd = _block_diag_ones()
    y = y_ref[...]
    mean = _head_sum(y, ones_bd) * (1.0 / HEAD_DIM)
    d = y - mean
    var = _head_sum(d * d, ones_bd) * (1.0 / HEAD_DIM)
    yn = d * lax.rsqrt(var + RWKV_GN_EPS) * lg_ref[...] + lb_ref[...]
    orw = ((yn + bonus_ref[...].astype(F32)) * g_ref[...].astype(F32)).astype(BF16)
    for j in range(nchunks):
        sl = slice(j * tn, (j + 1) * tn)
        o_ref[:, sl] = x_ref[:, sl] + g1_ref[:, sl] * (acc[j] + _dot(orw, wb_ref[:, sl]))


def _outproj(oc, y, bonus, g, lnx_g, lnx_b, w_out2, x, mod4, tm, tn):
    nb, t, d = x.shape
    cw, rw = oc.shape[2], y.shape[2]
    rows = lambda w: pl.BlockSpec((None, tm, w), lambda b, i: (b, i, 0))
    vec = pl.BlockSpec((1, rw), lambda b, i: (0, 0))
    return pl.pallas_call(
        functools.partial(_outproj_kernel, tn=tn),
        grid=(nb, t // tm),
        in_specs=[rows(cw), rows(rw), rows(rw), rows(rw), vec, vec,
                  pl.BlockSpec((None, cw, d), lambda b, i: (0, 0, 0), pipeline_mode=pl.Buffered(1)),
                  pl.BlockSpec((None, rw, d), lambda b, i: (1, 0, 0), pipeline_mode=pl.Buffered(1)),
                  rows(d),
                  pl.BlockSpec((None, None, 1, d), lambda b, i: (b, 2, 0, 0))],
        out_specs=rows(d),
        out_shape=jax.ShapeDtypeStruct((nb, t, d), F32),
        compiler_params=_cparams(("arbitrary", "arbitrary")),
        name="outproj",
    )(oc, y, bonus, g, lnx_g.reshape(1, rw), lnx_b.reshape(1, rw), w_out2, w_out2, x, mod4)


def _ffn_kernel(x_ref, g_ref, sc_ref, sh_ref, gate_ref, gf_ref, w1_ref, w3_ref, w2_ref,
                o_ref, h_ref, acc_ref, *, final_norm):
    j = pl.program_id(2)

    @pl.when(j == 0)
    def _():
        h_ref[...] = _modulated_norm(x_ref[...], g_ref[...], sc_ref[...], sh_ref[...]).astype(BF16)
        acc_ref[...] = jnp.zeros_like(acc_ref)

    h = h_ref[...]
    a = _dot(h, w1_ref[...])
    b = _dot(h, w3_ref[...])
    act = (a * _sigmoid(a) * b).astype(BF16)
    acc_ref[...] += _dot(act, w2_ref[...])

    @pl.when(j == pl.num_programs(2) - 1)
    def _():
        xo = x_ref[...] + gate_ref[...] * acc_ref[...]
        if final_norm:
            ms = jnp.mean(xo * xo, axis=-1, keepdims=True)
            xo = xo * lax.rsqrt(ms + RMS_EPS) * gf_ref[...]
        o_ref[...] = xo


def _ffn(x1, g, mod4, gf, w1, w3, w2, tm, tf, final_norm):
    nb, t, d = x1.shape
    f = w1.shape[1]
    modspec = lambda comp: pl.BlockSpec((None, None, 1, d), lambda b, i, j: (b, comp, 0, 0))
    vec = pl.BlockSpec((1, d), lambda b, i, j: (0, 0))
    return pl.pallas_call(
        functools.partial(_ffn_kernel, final_norm=final_norm),
        grid=(nb, t // tm, f // tf),
        in_specs=[pl.BlockSpec((None, tm, d), lambda b, i, j: (b, i, 0)),
                  vec, modspec(4), modspec(3), modspec(5), vec,
                  pl.BlockSpec((d, tf), lambda b, i, j: (0, j)),
                  pl.BlockSpec((d, tf), lambda b, i, j: (0, j)),
                  pl.BlockSpec((tf, d), lambda b, i, j: (j, 0))],
        out_specs=pl.BlockSpec((None, tm, d), lambda b, i, j: (b, i, 0)),
        out_shape=jax.ShapeDtypeStruct((nb, t, d), F32),
        scratch_shapes=[pltpu.VMEM((tm, d), BF16), pltpu.VMEM((tm, d), F32)],
        compiler_params=_cparams(("arbitrary", "arbitrary", "arbitrary")),
        name="ffn",
    )(x1, g.reshape(1, d), mod4, mod4, mod4, gf.reshape(1, d), w1, w3, w2)


def _pad_cols(w, n):
    return jnp.concatenate([w, jnp.zeros(w.shape[:-1] + (n - w.shape[-1],), w.dtype)], axis=-1)


def _tile(n, pref):
    return pref if n % pref == 0 else n


def kernel(x, c, w_ada, b_ada, norm1_g, w_in, tshift_mu, conv_w, conv_b, conv_gn_g, conv_gn_b,
           w0, w_w2, a0, a_w2, g_w2, k_k, k_a, r_k, lnx_g, lnx_b, w_out, norm2_g,
           w_ff1, w_ff3, w_ff2, norm_f_g):
    nb, t, d = x.shape
    depth = w_in.shape[0]
    cw = conv_w.shape[2]
    rw = w0.shape[1]
    n_rest = w_in.shape[2] - 2 * cw
    n_pad = -(-n_rest // LANES) * LANES
    assert rw % GROUP == 0 and cw % GROUP == 0

    for l in range(depth):
        mod, wib = _ada(c, w_ada[l], b_ada[l], w_in[l].T)
        mod4 = mod.reshape(nb, N_MOD, 1, d)
        oc, p, w1b, w3b, w2b, wob = _front(
            x, norm1_g[l], mod4, wib, conv_w[l], conv_b[l], conv_gn_g[l], conv_gn_b[l],
            cw, n_pad, _tile(t, 256), 512,
            cast_weights=((w_ff1[l], 1), (w_ff3[l], 1), (w_ff2[l], 2), (w_out[l], 1)))
        r, k, v, an, b, lw, g, bonus = _prep(p, tshift_mu[l], w0[l], w_w2[l], a0[l], a_w2[l], g_w2[l],
                                             k_k[l], k_a[l], r_k[l].reshape(-1), rw, _tile(t, 256))
        y = _wkv(r, k, v, an, b, lw, _tile(t, 256))
        x = _outproj(oc, y, bonus, g, lnx_g[l], lnx_b[l], wob.reshape(2, cw, d),
                     x, mod4, _tile(t, 512), 512)
        x = _ffn(x, norm2_g[l], mod4, norm_f_g, w1b, w3b, w2b,
                 _tile(t, 512), 512, final_norm=(l == depth - 1))
    return x
```

```python
import functools
import math

import jax
import jax.numpy as jnp
from jax import lax
from jax.experimental import pallas as pl
from jax.experimental.pallas import tpu as pltpu

F32 = jnp.float32
BF16 = jnp.bfloat16

HEAD_DIM = 64
CONV_KERNEL = 31
RMS_EPS = 1e-6
CONV_GN_EPS = 1e-5
RWKV_GN_EPS = 64e-5
N_MOD = 6

LANES = 128
SUBLANES = 8
BF16_ROWS = 16
MXU_DIM = 256
GROUP = MXU_DIM
CHUNK = 64
HALO = 32
VMEM_LIMIT = 52 * 1024 * 1024


def _cparams(sem):
    return pltpu.CompilerParams(dimension_semantics=sem, vmem_limit_bytes=VMEM_LIMIT)


def _dot(a, b):
    return jnp.dot(a, b, preferred_element_type=F32)


def _dot_nt(a, b):
    return lax.dot_general(a, b, (((1,), (1,)), ((), ())), preferred_element_type=F32)


def _dot_tn(a, b):
    return lax.dot_general(a, b, (((0,), (0,)), ((), ())), preferred_element_type=F32)


def _split2(x):
    hi = x.astype(BF16)
    lo = (x - hi.astype(F32)).astype(BF16)
    return hi, lo


def _sigmoid(x):
    return 1.0 / (1.0 + jnp.exp(-x))


def _head_sum(x, ones_bd):
    outs = [_dot(x[:, g * GROUP:(g + 1) * GROUP].astype(BF16), ones_bd)
            for g in range(x.shape[1] // GROUP)]
    return outs[0] if len(outs) == 1 else jnp.concatenate(outs, axis=1)


def _block_diag_ones():
    r = lax.broadcasted_iota(jnp.int32, (GROUP, GROUP), 0) // HEAD_DIM
    c = lax.broadcasted_iota(jnp.int32, (GROUP, GROUP), 1) // HEAD_DIM
    return jnp.where(r == c, 1.0, 0.0).astype(BF16)


def _ada_kernel(cb_ref, w_ref, b_ref, wi_ref, o_ref, wib_ref):
    wib_ref[...] = wi_ref[...].astype(wib_ref.dtype)
    nb = cb_ref.shape[0]
    tn = w_ref.shape[1]
    for b in range(nb):
        cb = cb_ref[b]
        sc = cb * _sigmoid(cb)
        for j in range(tn // LANES):
            sl = slice(j * LANES, (j + 1) * LANES)
            s = jnp.sum(w_ref[:, sl] * sc, axis=0, keepdims=True)
            o_ref[b:b + 1, sl] = s + b_ref[:, sl]


def _ada(c, w_ada, b_ada, w_in_t):
    nb, d = c.shape
    n = w_ada.shape[1]
    tn = 1024
    steps = n // tn
    rows = w_in_t.shape[0] // steps
    assert n % tn == 0 and rows * steps == w_in_t.shape[0] and rows % BF16_ROWS == 0
    cb = jnp.broadcast_to(c[:, :, None], (nb, d, LANES))
    wi_spec = pl.BlockSpec((rows, w_in_t.shape[1]), lambda j: (j, 0))
    return pl.pallas_call(
        _ada_kernel,
        grid=(steps,),
        in_specs=[pl.BlockSpec((nb, d, LANES), lambda j: (0, 0, 0)),
                  pl.BlockSpec((d, tn), lambda j: (0, j)),
                  pl.BlockSpec((1, tn), lambda j: (0, j)),
                  wi_spec],
        out_specs=[pl.BlockSpec((nb, tn), lambda j: (0, j)), wi_spec],
        out_shape=[jax.ShapeDtypeStruct((nb, n), F32), jax.ShapeDtypeStruct(w_in_t.shape, BF16)],
        compiler_params=_cparams(("arbitrary",)),
        name="ada",
    )(cb, w_ada, b_ada.reshape(1, n), w_in_t)


def _modulated_norm(x, g, sc, sh):
    ms = jnp.mean(x * x, axis=-1, keepdims=True)
    return (x * lax.rsqrt(ms + RMS_EPS) * g) * (1.0 + sc) + sh


def _front_kernel(x_ref, g_ref, sc_ref, sh_ref, wt_ref, cw_ref, cb_ref, gg_ref, gb_ref,
                  wf1_ref, wf3_ref, wf2_ref, wo_ref,
                  oc_ref, p_ref, wf1b_ref, wf3b_ref, wf2b_ref, wob_ref,
                  ush_ref, acc_ref, carry_ref, *, tn):
    tm = x_ref.shape[0]
    cw = oc_ref.shape[1]
    nsh = ush_ref.shape[1]
    n_rest = wt_ref.shape[0] - 2 * cw
    first = pl.program_id(1) == 0
    h = _modulated_norm(x_ref[...], g_ref[...], sc_ref[...], sh_ref[...]).astype(BF16)

    @pl.when(first)
    def _():
        carry_ref[...] = jnp.zeros_like(carry_ref)

    ush_ref[0, 0:HALO, :] = carry_ref[...]
    for j in range(cw // tn):
        sl = slice(j * tn, (j + 1) * tn)
        val = _dot_nt(h, wt_ref[sl, :])
        gate = _dot_nt(h, wt_ref[cw + j * tn:cw + (j + 1) * tn, :])
        ush_ref[0, HALO:HALO + tm, sl] = val * _sigmoid(gate)
    for src, dst in ((wf1_ref, wf1b_ref), (wf3_ref, wf3b_ref), (wf2_ref, wf2b_ref), (wo_ref, wob_ref)):
        dst[...] = src[...].astype(dst.dtype)
    carry_ref[...] = ush_ref[0, tm:tm + HALO, :]
    for phi in range(1, SUBLANES):
        ush_ref[phi, 0:nsh - SUBLANES, :] = ush_ref[0, phi:phi + nsh - SUBLANES, :]

    for lo in range(0, n_rest, tn):
        hi = min(lo + tn, n_rest)
        p_ref[:, lo:hi] = _dot_nt(h, wt_ref[2 * cw + lo:2 * cw + hi, :]).astype(p_ref.dtype)
    if p_ref.shape[1] > n_rest:
        p_ref[:, n_rest:] = jnp.zeros((tm, p_ref.shape[1] - n_rest), p_ref.dtype)

    rb = 64
    base = HALO - (CONV_KERNEL - 1)
    for c in range(cw // LANES):
        sl = slice(c * LANES, (c + 1) * LANES)
        for r in range(tm // rb):
            acc = jnp.zeros((rb, LANES), F32) + cb_ref[:, sl]
            for j in range(CONV_KERNEL):
                q, phi = divmod(base + j, SUBLANES)
                lo = r * rb + q * SUBLANES
                acc = acc + cw_ref[j:j + 1, sl] * ush_ref[phi, lo:lo + rb, sl]
            acc_ref[r * rb:(r + 1) * rb, sl] = acc

    ones_bd = _block_diag_ones()
    u = acc_ref[...]
    mean = _head_sum(u, ones_bd) * (1.0 / HEAD_DIM)
    d = u - mean
    var = _head_sum(d * d, ones_bd) * (1.0 / HEAD_DIM)
    y = d * lax.rsqrt(var + CONV_GN_EPS) * gg_ref[...] + gb_ref[...]
    oc_ref[...] = (y * _sigmoid(y)).astype(oc_ref.dtype)


def _front(x, g, mod4, wt, conv_w, conv_b, gn_g, gn_b, cw, n_pad, tm, tn, cast_weights):
    nb, t, d = x.shape
    n = wt.shape[0]
    nt = t // tm
    row = lambda v: v.reshape(1, cw)
    vec = pl.BlockSpec((1, cw), lambda b, i: (0, 0))
    cast_specs = []
    for cwt, revisits in cast_weights:
        rows = cwt.shape[0] * revisits // (nb * nt)
        assert rows * nb * nt == cwt.shape[0] * revisits and rows % BF16_ROWS == 0
        cast_specs.append(pl.BlockSpec((rows, cwt.shape[1]),
                                       lambda b, i, rv=revisits: ((b * nt + i) // rv, 0)))
    return pl.pallas_call(
        functools.partial(_front_kernel, tn=tn),
        grid=(nb, t // tm),
        in_specs=[pl.BlockSpec((None, tm, d), lambda b, i: (b, i, 0)),
                  pl.BlockSpec((1, d), lambda b, i: (0, 0)),
                  pl.BlockSpec((None, None, 1, d), lambda b, i: (b, 1, 0, 0)),
                  pl.BlockSpec((None, None, 1, d), lambda b, i: (b, 0, 0, 0)),
                  pl.BlockSpec((n, d), lambda b, i: (0, 0), pipeline_mode=pl.Buffered(1)),
                  pl.BlockSpec((CONV_KERNEL, cw), lambda b, i: (0, 0)), vec, vec, vec] + cast_specs,
        out_specs=[pl.BlockSpec((None, tm, cw), lambda b, i: (b, i, 0)),
                   pl.BlockSpec((None, tm, n_pad), lambda b, i: (b, i, 0))] + cast_specs,
        out_shape=[jax.ShapeDtypeStruct((nb, t, cw), BF16),
                   jax.ShapeDtypeStruct((nb, t, n_pad), BF16)]
        + [jax.ShapeDtypeStruct(cwt.shape, BF16) for cwt, _ in cast_weights],
        scratch_shapes=[pltpu.VMEM((SUBLANES, tm + HALO, cw), F32), pltpu.VMEM((tm, cw), F32),
                        pltpu.VMEM((HALO, cw), F32)],
        compiler_params=_cparams(("arbitrary", "arbitrary")),
        name="front",
    )(x, g.reshape(1, d), mod4, mod4, wt, conv_w, row(conv_b), row(gn_g), row(gn_b),
      *[cwt for cwt, _ in cast_weights])


def _prep_kernel(pr_ref, pk_ref, pv_ref, pl_ref, hr_ref, hk_ref, hv_ref, hl_ref,
                 mu_rkv_ref, mu_lora_ref,
                 w0_ref, ww2_ref, a0_ref, aw2_ref, gw2_ref, kk_ref, ka_ref, rk_ref,
                 r_ref, k_ref, v_ref, an_ref, b_ref, lw_ref, g_ref, bonus_ref, *, win_w, win_a, win_g):
    tm, rw = r_ref.shape
    first = pl.program_id(1) == 0
    rows = lax.broadcasted_iota(jnp.int32, (tm, 1), 0)

    def shifted(cur_ref, halo_ref, mu):
        cur = cur_ref[...].astype(F32)
        last = jnp.where(first, 0.0, halo_ref[BF16_ROWS - 1:BF16_ROWS, :].astype(F32))
        prev = jnp.where(rows == 0, last, pltpu.roll(cur, 1, axis=0))
        return cur + (prev - cur) * mu

    r = shifted(pr_ref, hr_ref, mu_rkv_ref[:, :rw])
    k = shifted(pk_ref, hk_ref, mu_rkv_ref[:, rw:2 * rw])
    v = shifted(pv_ref, hv_ref, mu_rkv_ref[:, 2 * rw:])
    ql = shifted(pl_ref, hl_ref, mu_lora_ref[...])
    xw, xa, xg = (ql[:, lo:hi] for lo, hi in (win_w, win_a, win_g))

    z = w0_ref[...] + _dot(jnp.tanh(xw).astype(BF16), ww2_ref[...])
    lw_ref[...] = (-math.exp(-0.5)) * _sigmoid(z)
    a = _sigmoid(a0_ref[...] + _dot(xa.astype(BF16), aw2_ref[...]))
    g_ref[...] = _dot(_sigmoid(xg).astype(BF16), gw2_ref[...]).astype(g_ref.dtype)

    ones_bd = _block_diag_ones()
    kk = k * kk_ref[...]
    ss = _head_sum(kk * kk, ones_bd)
    kkn = kk * lax.rsqrt(jnp.maximum(ss, 1e-24))
    kmod = k * (1.0 + (a - 1.0) * ka_ref[...])
    r_ref[...] = r.astype(r_ref.dtype)
    k_ref[...] = kmod.astype(k_ref.dtype)
    v_ref[...] = v.astype(v_ref.dtype)
    an_ref[...] = (-kkn).astype(an_ref.dtype)
    b_ref[...] = (kkn * a).astype(b_ref.dtype)
    bonus_ref[...] = (_head_sum(r * kmod * rk_ref[...], ones_bd) * v).astype(bonus_ref.dtype)


def _lane_window(lo, hi):
    return lo // LANES * LANES, -(-hi // LANES) * LANES


def _window_weight(w, lo, hi):
    a0, a1 = _lane_window(lo, hi)
    out = jnp.zeros((a1 - a0, w.shape[1]), w.dtype)
    return out.at[lo - a0:hi - a0].set(w)


def _prep(p, mu, w0, w_w2, a0, a_w2, g_w2, k_k, k_a, r_k, rw, tm):
    nb, t, n_pad = p.shape
    nrkv = 3 * rw
    nl = n_pad - nrkv
    dl, al, gl = w_w2.shape[0], a_w2.shape[0], g_w2.shape[0]
    assert nrkv % nl == 0
    r_blk = 0
    lora_blk = nrkv // nl
    wins = (_lane_window(0, dl), _lane_window(dl, dl + al), _lane_window(dl + al, dl + al + gl))
    ww2 = _window_weight(w_w2, 0, dl).astype(BF16)
    aw2 = _window_weight(a_w2, dl, dl + al).astype(BF16)
    gw2 = _window_weight(g_w2, dl + al, dl + al + gl).astype(BF16)
    mu_rkv = mu[:nrkv]
    mu_lora = _pad_cols(mu[nrkv:], nl)
    hb = tm // BF16_ROWS
    row = lambda v_: v_.reshape(1, -1)
    full = lambda shape: pl.BlockSpec(shape, lambda b, i: (0,) * len(shape))
    cur = lambda w, blk: pl.BlockSpec((None, tm, w), lambda b, i: (b, i, blk))
    halo = lambda w, blk: pl.BlockSpec((None, BF16_ROWS, w),
                                       lambda b, i: (b, jnp.maximum(i * hb - 1, 0), blk))
    out_spec = pl.BlockSpec((None, tm, rw), lambda b, i: (b, i, 0))
    bf = jax.ShapeDtypeStruct((nb, t, rw), BF16)
    return pl.pallas_call(
        functools.partial(_prep_kernel, win_w=wins[0], win_a=wins[1], win_g=wins[2]),
        grid=(nb, t // tm),
        in_specs=[cur(rw, r_blk), cur(rw, r_blk + 1), cur(rw, r_blk + 2), cur(nl, lora_blk),
                  halo(rw, r_blk), halo(rw, r_blk + 1), halo(rw, r_blk + 2), halo(nl, lora_blk),
                  full((1, nrkv)), full((1, nl)),
                  full((1, rw)), full(ww2.shape), full((1, rw)), full(aw2.shape),
                  full(gw2.shape), full((1, rw)), full((1, rw)), full((1, rw))],
        out_specs=[out_spec] * 8,
        out_shape=[bf, bf, bf, bf, bf, jax.ShapeDtypeStruct((nb, t, rw), F32), bf, bf],
        compiler_params=_cparams(("arbitrary", "arbitrary")),
        name="prep",
    )(p, p, p, p, p, p, p, p, row(mu_rkv), row(mu_lora), row(w0), ww2, row(a0), aw2, gw2,
      row(k_k), row(k_a), row(r_k))


def _wkv_kernel(r_ref, k_ref, v_ref, an_ref, b_ref, lw_ref, y_ref, s_ref):
    nb, tb, rw = r_ref.shape
    ngroups = rw // GROUP
    reps = GROUP // CHUNK
    streams = [(bi, g) for bi in range(nb) for g in range(ngroups)]

    @pl.when(pl.program_id(0) == 0)
    def _():
        s_ref[...] = jnp.zeros_like(s_ref)

    row = lax.broadcasted_iota(jnp.int32, (GROUP, GROUP), 0)
    col = lax.broadcasted_iota(jnp.int32, (GROUP, GROUP), 1)
    bd = (row // HEAD_DIM) == (col // HEAD_DIM)
    bd16 = (row // HEAD_DIM).astype(F32).astype(BF16) == (col // HEAD_DIM).astype(F32).astype(BF16)
    t_idx = lax.broadcasted_iota(jnp.int32, (CHUNK, GROUP), 0)
    j_idx = lax.broadcasted_iota(jnp.int32, (CHUNK, GROUP), 1) % CHUNK
    tj_strict = t_idx > j_idx
    tj_incl = t_idx >= j_idx
    ident = jnp.where(t_idx == j_idx, 1.0, 0.0)
    tr = lax.broadcasted_iota(jnp.int32, (CHUNK, CHUNK), 0)
    tc = lax.broadcasted_iota(jnp.int32, (CHUNK, CHUNK), 1)
    ltri = jnp.where(tr >= tc, 1.0, 0.0).astype(BF16)
    nlev = int(math.log2(CHUNK)) - 1

    def expand(xc):
        xb = xc.astype(BF16)
        return jnp.where(bd16, jnp.concatenate([xb] * reps, axis=0), jnp.zeros((), BF16))

    def stack(a, b):
        return jnp.concatenate([a, b], axis=0).astype(BF16)

    def chunk_body(c, carry):
        rs = pl.ds(pl.multiple_of(c * CHUNK, CHUNK), CHUNK)
        ns = len(streams)
        vg, ar, kb, vexp, kbe, p_end = [], [], [], [], [], []
        for bi, g in streams:
            ls = slice(g * GROUP, (g + 1) * GROUP)
            lw = lw_ref[bi, rs, ls]
            hi, lo = _split2(lw)
            cum = _dot(ltri, hi) + _dot(ltri, lo)
            tail = cum[CHUNK - 1:CHUNK, :]
            e_neg = jnp.exp(-cum)
            e_tail = jnp.exp(tail - cum)
            k_ = k_ref[bi, rs, ls].astype(F32)
            b_ = b_ref[bi, rs, ls].astype(F32)
            v_ = v_ref[bi, rs, ls].astype(F32)
            vg.append(v_)
            ar.append(stack(an_ref[bi, rs, ls].astype(F32) * jnp.exp(cum - lw),
                            r_ref[bi, rs, ls].astype(F32) * jnp.exp(cum)))
            kb.append(jnp.concatenate([expand(k_ * e_neg), expand(b_ * e_neg)], axis=0))
            vexp.append(expand(v_))
            kbe.append(stack(k_ * e_tail, b_ * e_tail))
            p_end.append(jnp.exp(tail))

        amat = [_dot_nt(ar[i], kb[i]) for i in range(ns)]
        a_kc = [stack(jnp.where(tj_strict, amat[i][:CHUNK, :GROUP], 0.0),
                      jnp.where(tj_incl, amat[i][CHUNK:, :GROUP], 0.0)) for i in range(ns)]
        a_rb = [jnp.where(tj_incl, amat[i][CHUNK:, GROUP:], 0.0).astype(BF16) for i in range(ns)]
        xp = [jnp.where(tj_strict, amat[i][:CHUNK, GROUP:], 0.0) for i in range(ns)]

        t_inv = [ident + xp[i] for i in range(ns)]
        xp = [_dot(xp[i].astype(BF16), expand(xp[i])) for i in range(ns)]
        for lev in range(nlev):
            if lev < nlev - 1:
                both = [_dot(stack(xp[i], t_inv[i]), expand(xp[i])) for i in range(ns)]
                xp = [both[i][:CHUNK] for i in range(ns)]
                t_inv = [t_inv[i] + both[i][CHUNK:] for i in range(ns)]
            else:
                t_inv = [t_inv[i] + _dot(t_inv[i].astype(BF16), expand(xp[i])) for i in range(ns)]

        s_old = [s_ref[i] for i in range(ns)]
        ars = [_dot_nt(ar[i], s_old[i].astype(BF16)) for i in range(ns)]
        akv = [_dot(a_kc[i], vexp[i]) for i in range(ns)]
        z = [ars[i][:CHUNK] + akv[i][:CHUNK] for i in range(ns)]
        u = [_dot(t_inv[i].astype(BF16), expand(z[i])) for i in range(ns)]
        y = [ars[i][CHUNK:] + akv[i][CHUNK:] + _dot(a_rb[i], expand(u[i])) for i in range(ns)]
        ds = [_dot_tn(stack(vg[i], u[i]), kbe[i]) for i in range(ns)]
        for i, (bi, g) in enumerate(streams):
            y_ref[bi, rs, g * GROUP:(g + 1) * GROUP] = y[i]
            s_ref[i] = s_old[i] * p_end[i] + jnp.where(bd, ds[i], 0.0)
        return carry

    lax.fori_loop(0, tb // CHUNK, chunk_body, 0, unroll=True)


def _wkv(r, k, v, an, b, lw, tb):
    nb, t, rw = r.shape
    spec = pl.BlockSpec((nb, tb, rw), lambda i: (0, i, 0))
    return pl.pallas_call(
        _wkv_kernel,
        grid=(t // tb,),
        in_specs=[spec] * 6,
        out_specs=spec,
        out_shape=jax.ShapeDtypeStruct((nb, t, rw), F32),
        scratch_shapes=[pltpu.VMEM((nb * (rw // GROUP), GROUP, GROUP), F32)],
        compiler_params=_cparams(("arbitrary",)),
        name="wkv",
    )(r, k, v, an, b, lw)


def _outproj_kernel(oc_ref, y_ref, bonus_ref, g_ref, lg_ref, lb_ref, wa_ref, wb_ref, x_ref, g1_ref,
                    n2_ref, sc2_ref, sh2_ref, o_ref, h2_ref, *, tn, parts):
    tm, d = o_ref.shape
    nchunks = d // tn
    rows = [slice(k * (tm // parts), (k + 1) * (tm // parts)) for k in range(parts)]
    ones_bd = _block_diag_ones()
    acc = [[_dot(oc_ref[r, :], wa_ref[:, j * tn:(j + 1) * tn]) for j in range(nchunks)] for r in rows]
    orw = []
    for r in rows:
        y = y_ref[r, :]
        mean = _head_sum(y, ones_bd) * (1.0 / HEAD_DIM)
        dv = y - mean
        var = _head_sum(dv * dv, ones_bd) * (1.0 / HEAD_DIM)
        yn = dv * lax.rsqrt(var + RWKV_GN_EPS) * lg_ref[...] + lb_ref[...]
        orw.append(((yn + bonus_ref[r, :].astype(F32)) * g_ref[r, :].astype(F32)).astype(BF16))
    for k, r in enumerate(rows):
        for j in range(nchunks):
            sl = slice(j * tn, (j + 1) * tn)
            o_ref[r, sl] = x_ref[r, sl] + g1_ref[:, sl] * (acc[k][j] + _dot(orw[k], wb_ref[:, sl]))
    for r in rows:
        h2_ref[r, :] = _modulated_norm(o_ref[r, :], n2_ref[...], sc2_ref[...], sh2_ref[...]).astype(BF16)


def _outproj(oc, y, bonus, g, lnx_g, lnx_b, w_out2, x, mod4, norm2_g, tm, tn):
    nb, t, d = x.shape
    cw, rw = oc.shape[2], y.shape[2]
    rows = lambda w: pl.BlockSpec((None, tm, w), lambda b, i: (b, i, 0))
    vec = lambda w: pl.BlockSpec((1, w), lambda b, i: (0, 0))
    modspec = lambda comp: pl.BlockSpec((None, None, 1, d), lambda b, i: (b, comp, 0, 0))
    return pl.pallas_call(
        functools.partial(_outproj_kernel, tn=tn, parts=2),
        grid=(nb, t // tm),
        in_specs=[rows(cw), rows(rw), rows(rw), rows(rw), vec(rw), vec(rw),
                  pl.BlockSpec((None, cw, d), lambda b, i: (0, 0, 0), pipeline_mode=pl.Buffered(1)),
                  pl.BlockSpec((None, rw, d), lambda b, i: (1, 0, 0), pipeline_mode=pl.Buffered(1)),
                  rows(d), modspec(2), vec(d), modspec(4), modspec(3)],
        out_specs=[rows(d), rows(d)],
        out_shape=[jax.ShapeDtypeStruct((nb, t, d), F32), jax.ShapeDtypeStruct((nb, t, d), BF16)],
        compiler_params=_cparams(("arbitrary", "arbitrary")),
        name="outproj",
    )(oc, y, bonus, g, lnx_g.reshape(1, rw), lnx_b.reshape(1, rw), w_out2, w_out2, x, mod4,
      norm2_g.reshape(1, d), mod4, mod4)


def _ffn_kernel(x_ref, h_ref, gate_ref, gf_ref, w1_ref, w3_ref, w2_ref, o_ref, acc_ref, *, final_norm):
    j = pl.program_id(2)

    @pl.when(j == 0)
    def _():
        acc_ref[...] = jnp.zeros_like(acc_ref)

    h = h_ref[...]
    a = _dot(h, w1_ref[...])
    b = _dot(h, w3_ref[...])
    act = (a * _sigmoid(a) * b).astype(BF16)
    acc_ref[...] += _dot(act, w2_ref[...])

    @pl.when(j == pl.num_programs(2) - 1)
    def _():
        xo = x_ref[...] + gate_ref[...] * acc_ref[...]
        if final_norm:
            ms = jnp.mean(xo * xo, axis=-1, keepdims=True)
            xo = xo * lax.rsqrt(ms + RMS_EPS) * gf_ref[...]
        o_ref[...] = xo


def _ffn(x1, h2, mod4, gf, w1, w3, w2, tm, tf, final_norm):
    nb, t, d = x1.shape
    f = w1.shape[1]
    rows = pl.BlockSpec((None, tm, d), lambda b, i, j: (b, i, 0))
    return pl.pallas_call(
        functools.partial(_ffn_kernel, final_norm=final_norm),
        grid=(nb, t // tm, f // tf),
        in_specs=[rows, rows,
                  pl.BlockSpec((None, None, 1, d), lambda b, i, j: (b, 5, 0, 0)),
                  pl.BlockSpec((1, d), lambda b, i, j: (0, 0)),
                  pl.BlockSpec((d, tf), lambda b, i, j: (0, j)),
                  pl.BlockSpec((d, tf), lambda b, i, j: (0, j)),
                  pl.BlockSpec((tf, d), lambda b, i, j: (j, 0))],
        out_specs=rows,
        out_shape=jax.ShapeDtypeStruct((nb, t, d), F32),
        scratch_shapes=[pltpu.VMEM((tm, d), F32)],
        compiler_params=_cparams(("arbitrary", "arbitrary", "arbitrary")),
        name="ffn",
    )(x1, h2, mod4, gf.reshape(1, d), w1, w3, w2)


def _pad_cols(w, n):
    return jnp.concatenate([w, jnp.zeros(w.shape[:-1] + (n - w.shape[-1],), w.dtype)], axis=-1)


def _tile(n, pref):
    return pref if n % pref == 0 else n


def kernel(x, c, w_ada, b_ada, norm1_g, w_in, tshift_mu, conv_w, conv_b, conv_gn_g, conv_gn_b,
           w0, w_w2, a0, a_w2, g_w2, k_k, k_a, r_k, lnx_g, lnx_b, w_out, norm2_g,
           w_ff1, w_ff3, w_ff2, norm_f_g):
    nb, t, d = x.shape
    depth = w_in.shape[0]
    cw = conv_w.shape[2]
    rw = w0.shape[1]
    n_rest = w_in.shape[2] - 2 * cw
    n_pad = -(-n_rest // LANES) * LANES
    assert rw % GROUP == 0 and cw % GROUP == 0

    for l in range(depth):
        mod, wib = _ada(c, w_ada[l], b_ada[l], w_in[l].T)
        mod4 = mod.reshape(nb, N_MOD, 1, d)
        oc, p, w1b, w3b, w2b, wob = _front(
            x, norm1_g[l], mod4, wib, conv_w[l], conv_b[l], conv_gn_g[l], conv_gn_b[l],
            cw, n_pad, _tile(t, 256), 512,
            cast_weights=((w_ff1[l], 1), (w_ff3[l], 1), (w_ff2[l], 2), (w_out[l], 1)))
        r, k, v, an, b, lw, g, bonus = _prep(p, tshift_mu[l], w0[l], w_w2[l], a0[l], a_w2[l], g_w2[l],
                                             k_k[l], k_a[l], r_k[l].reshape(-1), rw, _tile(t, 512))
        y = _wkv(r, k, v, an, b, lw, _tile(t, 256))
        x, h2 = _outproj(oc, y, bonus, g, lnx_g[l], lnx_b[l], wob.reshape(2, cw, d),
                         x, mod4, norm2_g[l], _tile(t, 512), 512)
        x = _ffn(x, h2, mod4, norm_f_g, w1b, w3b, w2b,
                 _tile(t, 512), 512, final_norm=(l == depth - 1))
    return x
```

```python
import functools
import math

import jax
import jax.numpy as jnp
from jax import lax
from jax.experimental import pallas as pl
from jax.experimental.pallas import tpu as pltpu

F32 = jnp.float32
BF16 = jnp.bfloat16

HEAD_DIM = 64
CONV_KERNEL = 31
RMS_EPS = 1e-6
CONV_GN_EPS = 1e-5
RWKV_GN_EPS = 64e-5
N_MOD = 6

LANES = 128
SUBLANES = 8
BF16_ROWS = 16
MXU_DIM = 256
GROUP = MXU_DIM
CHUNK = 64
HALO = 32
VMEM_LIMIT = 52 * 1024 * 1024


def _cparams(sem):
    return pltpu.CompilerParams(dimension_semantics=sem, vmem_limit_bytes=VMEM_LIMIT)


def _dot(a, b):
    return jnp.dot(a, b, preferred_element_type=F32)


def _dot_nt(a, b):
    return lax.dot_general(a, b, (((1,), (1,)), ((), ())), preferred_element_type=F32)


def _dot_tn(a, b):
    return lax.dot_general(a, b, (((0,), (0,)), ((), ())), preferred_element_type=F32)


def _split2(x):
    hi = x.astype(BF16)
    lo = (x - hi.astype(F32)).astype(BF16)
    return hi, lo


def _sigmoid(x):
    return 1.0 / (1.0 + jnp.exp(-x))


def _head_sum(x, ones_bd):
    outs = [_dot(x[:, g * GROUP:(g + 1) * GROUP].astype(BF16), ones_bd)
            for g in range(x.shape[1] // GROUP)]
    return outs[0] if len(outs) == 1 else jnp.concatenate(outs, axis=1)


def _block_diag_ones():
    r = lax.broadcasted_iota(jnp.int32, (GROUP, GROUP), 0) // HEAD_DIM
    c = lax.broadcasted_iota(jnp.int32, (GROUP, GROUP), 1) // HEAD_DIM
    return jnp.where(r == c, 1.0, 0.0).astype(BF16)


def _ada_kernel(cb_ref, w_ref, b_ref, wit_ref, o_ref, wib_ref):
    n_cols = wit_ref.shape[0]
    for r in range(0, n_cols, LANES):
        hi = min(r + LANES, n_cols)
        wib_ref[:, r:hi] = wit_ref[r:hi, :].T.astype(wib_ref.dtype)
    nb = cb_ref.shape[0]
    tn = w_ref.shape[1]
    for b in range(nb):
        cb = cb_ref[b]
        sc = cb * _sigmoid(cb)
        for j in range(tn // LANES):
            sl = slice(j * LANES, (j + 1) * LANES)
            s = jnp.sum(w_ref[:, sl] * sc, axis=0, keepdims=True)
            o_ref[b:b + 1, sl] = s + b_ref[:, sl]


def _ada(c, w_ada, b_ada, w_in_t):
    nb, d = c.shape
    n = w_ada.shape[1]
    steps = w_in_t.shape[1] // LANES
    tn = n // steps
    assert steps * LANES == w_in_t.shape[1] and tn * steps == n and tn % LANES == 0
    cb = jnp.broadcast_to(c[:, :, None], (nb, d, LANES))
    return pl.pallas_call(
        _ada_kernel,
        grid=(steps,),
        in_specs=[pl.BlockSpec((nb, d, LANES), lambda j: (0, 0, 0)),
                  pl.BlockSpec((d, tn), lambda j: (0, j)),
                  pl.BlockSpec((1, tn), lambda j: (0, j)),
                  pl.BlockSpec((w_in_t.shape[0], LANES), lambda j: (0, j))],
        out_specs=[pl.BlockSpec((nb, tn), lambda j: (0, j)),
                   pl.BlockSpec((LANES, w_in_t.shape[0]), lambda j: (j, 0))],
        out_shape=[jax.ShapeDtypeStruct((nb, n), F32),
                   jax.ShapeDtypeStruct(w_in_t.shape[::-1], BF16)],
        compiler_params=_cparams(("arbitrary",)),
        name="ada",
    )(cb, w_ada, b_ada.reshape(1, n), w_in_t)


def _modulated_norm(x, g, sc, sh):
    ms = jnp.mean(x * x, axis=-1, keepdims=True)
    return (x * lax.rsqrt(ms + RMS_EPS) * g) * (1.0 + sc) + sh


def _front_kernel(x_ref, g_ref, sc_ref, sh_ref, w_ref, cw_ref, cb_ref, gg_ref, gb_ref,
                  wf1_ref, wf3_ref, wf2_ref, wo_ref,
                  oc_ref, p_ref, wf1b_ref, wf3b_ref, wf2b_ref, wob_ref,
                  ush_ref, acc_ref, carry_ref, *, tn):
    tm = x_ref.shape[0]
    cw = oc_ref.shape[1]
    nsh = ush_ref.shape[1]
    n_rest = w_ref.shape[1] - 2 * cw
    first = pl.program_id(1) == 0
    h = _modulated_norm(x_ref[...], g_ref[...], sc_ref[...], sh_ref[...]).astype(BF16)

    @pl.when(first)
    def _():
        carry_ref[...] = jnp.zeros_like(carry_ref)

    ush_ref[0, 0:HALO, :] = carry_ref[...]
    for j in range(cw // tn):
        sl = slice(j * tn, (j + 1) * tn)
        val = _dot(h, w_ref[:, sl])
        gate = _dot(h, w_ref[:, cw + j * tn:cw + (j + 1) * tn])
        ush_ref[0, HALO:HALO + tm, sl] = val * _sigmoid(gate)
    for src, dst in ((wf1_ref, wf1b_ref), (wf3_ref, wf3b_ref), (wf2_ref, wf2b_ref), (wo_ref, wob_ref)):
        dst[...] = src[...].astype(dst.dtype)
    carry_ref[...] = ush_ref[0, tm:tm + HALO, :]
    for phi in range(1, SUBLANES):
        ush_ref[phi, 0:nsh - SUBLANES, :] = ush_ref[0, phi:phi + nsh - SUBLANES, :]

    for lo in range(0, n_rest, tn):
        hi = min(lo + tn, n_rest)
        p_ref[:, lo:hi] = _dot(h, w_ref[:, 2 * cw + lo:2 * cw + hi]).astype(p_ref.dtype)
    if p_ref.shape[1] > n_rest:
        p_ref[:, n_rest:] = jnp.zeros((tm, p_ref.shape[1] - n_rest), p_ref.dtype)

    rb = 64
    base = HALO - (CONV_KERNEL - 1)
    for c in range(cw // LANES):
        sl = slice(c * LANES, (c + 1) * LANES)
        for r in range(tm // rb):
            acc = jnp.zeros((rb, LANES), F32) + cb_ref[:, sl]
            for j in range(CONV_KERNEL):
                q, phi = divmod(base + j, SUBLANES)
                lo = r * rb + q * SUBLANES
                acc = acc + cw_ref[j:j + 1, sl] * ush_ref[phi, lo:lo + rb, sl]
            acc_ref[r * rb:(r + 1) * rb, sl] = acc

    ones_bd = _block_diag_ones()
    u = acc_ref[...]
    mean = _head_sum(u, ones_bd) * (1.0 / HEAD_DIM)
    d = u - mean
    var = _head_sum(d * d, ones_bd) * (1.0 / HEAD_DIM)
    y = d * lax.rsqrt(var + CONV_GN_EPS) * gg_ref[...] + gb_ref[...]
    oc_ref[...] = (y * _sigmoid(y)).astype(oc_ref.dtype)


def _front(x, g, mod4, w, conv_w, conv_b, gn_g, gn_b, cw, n_pad, tm, tn, cast_weights):
    nb, t, d = x.shape
    n = w.shape[1]
    nt = t // tm
    row = lambda v: v.reshape(1, cw)
    vec = pl.BlockSpec((1, cw), lambda b, i: (0, 0))
    cast_specs = []
    for cwt, revisits in cast_weights:
        rows = cwt.shape[0] * revisits // (nb * nt)
        assert rows * nb * nt == cwt.shape[0] * revisits and rows % BF16_ROWS == 0
        cast_specs.append(pl.BlockSpec((rows, cwt.shape[1]),
                                       lambda b, i, rv=revisits: ((b * nt + i) // rv, 0)))
    return pl.pallas_call(
        functools.partial(_front_kernel, tn=tn),
        grid=(nb, t // tm),
        in_specs=[pl.BlockSpec((None, tm, d), lambda b, i: (b, i, 0)),
                  pl.BlockSpec((1, d), lambda b, i: (0, 0)),
                  pl.BlockSpec((None, None, 1, d), lambda b, i: (b, 1, 0, 0)),
                  pl.BlockSpec((None, None, 1, d), lambda b, i: (b, 0, 0, 0)),
                  pl.BlockSpec((d, n), lambda b, i: (0, 0), pipeline_mode=pl.Buffered(1)),
                  pl.BlockSpec((CONV_KERNEL, cw), lambda b, i: (0, 0)), vec, vec, vec] + cast_specs,
        out_specs=[pl.BlockSpec((None, tm, cw), lambda b, i: (b, i, 0)),
                   pl.BlockSpec((None, tm, n_pad), lambda b, i: (b, i, 0))] + cast_specs,
        out_shape=[jax.ShapeDtypeStruct((nb, t, cw), BF16),
                   jax.ShapeDtypeStruct((nb, t, n_pad), BF16)]
        + [jax.ShapeDtypeStruct(cwt.shape, BF16) for cwt, _ in cast_weights],
        scratch_shapes=[pltpu.VMEM((SUBLANES, tm + HALO, cw), F32), pltpu.VMEM((tm, cw), F32),
                        pltpu.VMEM((HALO, cw), F32)],
        compiler_params=_cparams(("arbitrary", "arbitrary")),
        name="front",
    )(x, g.reshape(1, d), mod4, mod4, w, conv_w, row(conv_b), row(gn_g), row(gn_b),
      *[cwt for cwt, _ in cast_weights])


def _prep_kernel(pr_ref, pk_ref, pv_ref, pl_ref, hr_ref, hk_ref, hv_ref, hl_ref,
                 mu_rkv_ref, mu_lora_ref,
                 w0_ref, ww2_ref, a0_ref, aw2_ref, gw2_ref, kk_ref, ka_ref, rk_ref,
                 r_ref, k_ref, v_ref, an_ref, b_ref, lw_ref, g_ref, bonus_ref, *, win_w, win_a, win_g):
    tm, rw = r_ref.shape
    first = pl.program_id(1) == 0
    rows = lax.broadcasted_iota(jnp.int32, (tm, 1), 0)

    def shifted(cur_ref, halo_ref, mu):
        cur = cur_ref[...].astype(F32)
        last = jnp.where(first, 0.0, halo_ref[BF16_ROWS - 1:BF16_ROWS, :].astype(F32))
        prev = jnp.where(rows == 0, last, pltpu.roll(cur, 1, axis=0))
        return cur + (prev - cur) * mu

    r = shifted(pr_ref, hr_ref, mu_rkv_ref[:, :rw])
    k = shifted(pk_ref, hk_ref, mu_rkv_ref[:, rw:2 * rw])
    v = shifted(pv_ref, hv_ref, mu_rkv_ref[:, 2 * rw:])
    ql = shifted(pl_ref, hl_ref, mu_lora_ref[...])
    xw, xa, xg = (ql[:, lo:hi] for lo, hi in (win_w, win_a, win_g))

    z = w0_ref[...] + _dot(jnp.tanh(xw).astype(BF16), ww2_ref[...])
    lw_ref[...] = (-math.exp(-0.5)) * _sigmoid(z)
    a = _sigmoid(a0_ref[...] + _dot(xa.astype(BF16), aw2_ref[...]))
    g_ref[...] = _dot(_sigmoid(xg).astype(BF16), gw2_ref[...]).astype(g_ref.dtype)

    ones_bd = _block_diag_ones()
    kk = k * kk_ref[...]
    ss = _head_sum(kk * kk, ones_bd)
    kkn = kk * lax.rsqrt(jnp.maximum(ss, 1e-24))
    kmod = k * (1.0 + (a - 1.0) * ka_ref[...])
    r_ref[...] = r.astype(r_ref.dtype)
    k_ref[...] = kmod.astype(k_ref.dtype)
    v_ref[...] = v.astype(v_ref.dtype)
    an_ref[...] = (-kkn).astype(an_ref.dtype)
    b_ref[...] = (kkn * a).astype(b_ref.dtype)
    bonus_ref[...] = (_head_sum(r * kmod * rk_ref[...], ones_bd) * v).astype(bonus_ref.dtype)


def _lane_window(lo, hi):
    return lo // LANES * LANES, -(-hi // LANES) * LANES


def _window_weight(w, lo, hi):
    a0, a1 = _lane_window(lo, hi)
    out = jnp.zeros((a1 - a0, w.shape[1]), w.dtype)
    return out.at[lo - a0:hi - a0].set(w)


def _prep(p, mu, w0, w_w2, a0, a_w2, g_w2, k_k, k_a, r_k, rw, tm):
    nb, t, n_pad = p.shape
    nrkv = 3 * rw
    nl = n_pad - nrkv
    dl, al, gl = w_w2.shape[0], a_w2.shape[0], g_w2.shape[0]
    assert nrkv % nl == 0
    r_blk = 0
    lora_blk = nrkv // nl
    wins = (_lane_window(0, dl), _lane_window(dl, dl + al), _lane_window(dl + al, dl + al + gl))
    ww2 = _window_weight(w_w2, 0, dl).astype(BF16)
    aw2 = _window_weight(a_w2, dl, dl + al).astype(BF16)
    gw2 = _window_weight(g_w2, dl + al, dl + al + gl).astype(BF16)
    mu_rkv = mu[:nrkv]
    mu_lora = _pad_cols(mu[nrkv:], nl)
    hb = tm // BF16_ROWS
    row = lambda v_: v_.reshape(1, -1)
    full = lambda shape: pl.BlockSpec(shape, lambda b, i: (0,) * len(shape))
    cur = lambda w, blk: pl.BlockSpec((None, tm, w), lambda b, i: (b, i, blk))
    halo = lambda w, blk: pl.BlockSpec((None, BF16_ROWS, w),
                                       lambda b, i: (b, jnp.maximum(i * hb - 1, 0), blk))
    out_spec = pl.BlockSpec((None, tm, rw), lambda b, i: (b, i, 0))
    bf = jax.ShapeDtypeStruct((nb, t, rw), BF16)
    return pl.pallas_call(
        functools.partial(_prep_kernel, win_w=wins[0], win_a=wins[1], win_g=wins[2]),
        grid=(nb, t // tm),
        in_specs=[cur(rw, r_blk), cur(rw, r_blk + 1), cur(rw, r_blk + 2), cur(nl, lora_blk),
                  halo(rw, r_blk), halo(rw, r_blk + 1), halo(rw, r_blk + 2), halo(nl, lora_blk),
                  full((1, nrkv)), full((1, nl)),
                  full((1, rw)), full(ww2.shape), full((1, rw)), full(aw2.shape),
                  full(gw2.shape), full((1, rw)), full((1, rw)), full((1, rw))],
        out_specs=[out_spec] * 8,
        out_shape=[bf, bf, bf, bf, bf, jax.ShapeDtypeStruct((nb, t, rw), F32), bf, bf],
        compiler_params=_cparams(("arbitrary", "arbitrary")),
        name="prep",
    )(p, p, p, p, p, p, p, p, row(mu_rkv), row(mu_lora), row(w0), ww2, row(a0), aw2, gw2,
      row(k_k), row(k_a), row(r_k))


def _wkv_kernel(r_ref, k_ref, v_ref, an_ref, b_ref, lw_ref, y_ref, s_ref):
    nb, tb, rw = r_ref.shape
    ngroups = rw // GROUP
    reps = GROUP // CHUNK
    streams = [(bi, g) for bi in range(nb) for g in range(ngroups)]

    @pl.when(pl.program_id(0) == 0)
    def _():
        s_ref[...] = jnp.zeros_like(s_ref)

    row = lax.broadcasted_iota(jnp.int32, (GROUP, GROUP), 0)
    col = lax.broadcasted_iota(jnp.int32, (GROUP, GROUP), 1)
    bd = (row // HEAD_DIM) == (col // HEAD_DIM)
    bd16 = (row // HEAD_DIM).astype(F32).astype(BF16) == (col // HEAD_DIM).astype(F32).astype(BF16)
    t_idx = lax.broadcasted_iota(jnp.int32, (CHUNK, GROUP), 0)
    j_idx = lax.broadcasted_iota(jnp.int32, (CHUNK, GROUP), 1) % CHUNK
    tj_strict = t_idx > j_idx
    tj_incl = t_idx >= j_idx
    ident = jnp.where(t_idx == j_idx, 1.0, 0.0)
    tr = lax.broadcasted_iota(jnp.int32, (CHUNK, CHUNK), 0)
    tc = lax.broadcasted_iota(jnp.int32, (CHUNK, CHUNK), 1)
    ltri = jnp.where(tr >= tc, 1.0, 0.0).astype(BF16)
    nlev = int(math.log2(CHUNK)) - 1

    def expand(xc):
        xb = xc.astype(BF16)
        return jnp.where(bd16, jnp.concatenate([xb] * reps, axis=0), jnp.zeros((), BF16))

    def stack(a, b):
        return jnp.concatenate([a, b], axis=0).astype(BF16)

    def chunk_body(c, carry):
        rs = pl.ds(pl.multiple_of(c * CHUNK, CHUNK), CHUNK)
        ns = len(streams)
        vg, ar, kb, vexp, kbe, p_end = [], [], [], [], [], []
        for bi, g in streams:
            ls = slice(g * GROUP, (g + 1) * GROUP)
            lw = lw_ref[bi, rs, ls]
            hi, lo = _split2(lw)
            cum = _dot(ltri, hi) + _dot(ltri, lo)
            tail = cum[CHUNK - 1:CHUNK, :]
            e_neg = jnp.exp(-cum)
            e_tail = jnp.exp(tail - cum)
            k_ = k_ref[bi, rs, ls].astype(F32)
            b_ = b_ref[bi, rs, ls].astype(F32)
            v_ = v_ref[bi, rs, ls].astype(F32)
            vg.append(v_)
            ar.append(stack(an_ref[bi, rs, ls].astype(F32) * jnp.exp(cum - lw),
                            r_ref[bi, rs, ls].astype(F32) * jnp.exp(cum)))
            kb.append(jnp.concatenate([expand(k_ * e_neg), expand(b_ * e_neg)], axis=0))
            vexp.append(expand(v_))
            kbe.append(stack(k_ * e_tail, b_ * e_tail))
            p_end.append(jnp.exp(tail))

        amat = [_dot_nt(ar[i], kb[i]) for i in range(ns)]
        a_kc = [stack(jnp.where(tj_strict, amat[i][:CHUNK, :GROUP], 0.0),
                      jnp.where(tj_incl, amat[i][CHUNK:, :GROUP], 0.0)) for i in range(ns)]
        a_rb = [jnp.where(tj_incl, amat[i][CHUNK:, GROUP:], 0.0).astype(BF16) for i in range(ns)]
        xp = [jnp.where(tj_strict, amat[i][:CHUNK, GROUP:], 0.0) for i in range(ns)]

        t_inv = [ident + xp[i] for i in range(ns)]
        xp = [_dot(xp[i].astype(BF16), expand(xp[i])) for i in range(ns)]
        for lev in range(nlev):
            if lev < nlev - 1:
                both = [_dot(stack(xp[i], t_inv[i]), expand(xp[i])) for i in range(ns)]
                xp = [both[i][:CHUNK] for i in range(ns)]
                t_inv = [t_inv[i] + both[i][CHUNK:] for i in range(ns)]
            else:
                t_inv = [t_inv[i] + _dot(t_inv[i].astype(BF16), expand(xp[i])) for i in range(ns)]

        s_old = [s_ref[i] for i in range(ns)]
        ars = [_dot_nt(ar[i], s_old[i].astype(BF16)) for i in range(ns)]
        akv = [_dot(a_kc[i], vexp[i]) for i in range(ns)]
        z = [ars[i][:CHUNK] + akv[i][:CHUNK] for i in range(ns)]
        u = [_dot(t_inv[i].astype(BF16), expand(z[i])) for i in range(ns)]
        y = [ars[i][CHUNK:] + akv[i][CHUNK:] + _dot(a_rb[i], expand(u[i])) for i in range(ns)]
        ds = [_dot_tn(stack(vg[i], u[i]), kbe[i]) for i in range(ns)]
        for i, (bi, g) in enumerate(streams):
            y_ref[bi, rs, g * GROUP:(g + 1) * GROUP] = y[i]
            s_ref[i] = s_old[i] * p_end[i] + jnp.where(bd, ds[i], 0.0)
        return carry

    lax.fori_loop(0, tb // CHUNK, chunk_body, 0, unroll=True)


def _wkv(r, k, v, an, b, lw, tb):
    nb, t, rw = r.shape
    spec = pl.BlockSpec((nb, tb, rw), lambda i: (0, i, 0))
    return pl.pallas_call(
        _wkv_kernel,
        grid=(t // tb,),
        in_specs=[spec] * 6,
        out_specs=spec,
        out_shape=jax.ShapeDtypeStruct((nb, t, rw), F32),
        scratch_shapes=[pltpu.VMEM((nb * (rw // GROUP), GROUP, GROUP), F32)],
        compiler_params=_cparams(("arbitrary",)),
        name="wkv",
    )(r, k, v, an, b, lw)


def _outproj_kernel(oc_ref, y_ref, bonus_ref, g_ref, lg_ref, lb_ref, wa_ref, wb_ref, x_ref, g1_ref,
                    n2_ref, sc2_ref, sh2_ref, o_ref, h2_ref, *, tn, parts):
    tm, d = o_ref.shape
    nchunks = d // tn
    rows = [slice(k * (tm // parts), (k + 1) * (tm // parts)) for k in range(parts)]
    ones_bd = _block_diag_ones()
    acc = [[_dot(oc_ref[r, :], wa_ref[:, j * tn:(j + 1) * tn]) for j in range(nchunks)] for r in rows]
    orw = []
    for r in rows:
        y = y_ref[r, :]
        mean = _head_sum(y, ones_bd) * (1.0 / HEAD_DIM)
        dv = y - mean
        var = _head_sum(dv * dv, ones_bd) * (1.0 / HEAD_DIM)
        yn = dv * lax.rsqrt(var + RWKV_GN_EPS) * lg_ref[...] + lb_ref[...]
        orw.append(((yn + bonus_ref[r, :].astype(F32)) * g_ref[r, :].astype(F32)).astype(BF16))
    for k, r in enumerate(rows):
        for j in range(nchunks):
            sl = slice(j * tn, (j + 1) * tn)
            o_ref[r, sl] = x_ref[r, sl] + g1_ref[:, sl] * (acc[k][j] + _dot(orw[k], wb_ref[:, sl]))
    for r in rows:
        h2_ref[r, :] = _modulated_norm(o_ref[r, :], n2_ref[...], sc2_ref[...], sh2_ref[...]).astype(BF16)


def _outproj(oc, y, bonus, g, lnx_g, lnx_b, w_out2, x, mod4, norm2_g, tm, tn):
    nb, t, d = x.shape
    cw, rw = oc.shape[2], y.shape[2]
    rows = lambda w: pl.BlockSpec((None, tm, w), lambda b, i: (b, i, 0))
    vec = lambda w: pl.BlockSpec((1, w), lambda b, i: (0, 0))
    modspec = lambda comp: pl.BlockSpec((None, None, 1, d), lambda b, i: (b, comp, 0, 0))
    return pl.pallas_call(
        functools.partial(_outproj_kernel, tn=tn, parts=2),
        grid=(nb, t // tm),
        in_specs=[rows(cw), rows(rw), rows(rw), rows(rw), vec(rw), vec(rw),
                  pl.BlockSpec((None, cw, d), lambda b, i: (0, 0, 0), pipeline_mode=pl.Buffered(1)),
                  pl.BlockSpec((None, rw, d), lambda b, i: (1, 0, 0), pipeline_mode=pl.Buffered(1)),
                  rows(d), modspec(2), vec(d), modspec(4), modspec(3)],
        out_specs=[rows(d), rows(d)],
        out_shape=[jax.ShapeDtypeStruct((nb, t, d), F32), jax.ShapeDtypeStruct((nb, t, d), BF16)],
        compiler_params=_cparams(("arbitrary", "arbitrary")),
        name="outproj",
    )(oc, y, bonus, g, lnx_g.reshape(1, rw), lnx_b.reshape(1, rw), w_out2, w_out2, x, mod4,
      norm2_g.reshape(1, d), mod4, mod4)


def _ffn_kernel(x_ref, h_ref, gate_ref, gf_ref, w1_ref, w3_ref, w2_ref, o_ref, acc_ref, *, final_norm):
    j = pl.program_id(2)

    @pl.when(j == 0)
    def _():
        acc_ref[...] = jnp.zeros_like(acc_ref)

    h = h_ref[...]
    a = _dot(h, w1_ref[...])
    b = _dot(h, w3_ref[...])
    act = (a * _sigmoid(a) * b).astype(BF16)
    acc_ref[...] += _dot(act, w2_ref[...])

    @pl.when(j == pl.num_programs(2) - 1)
    def _():
        xo = x_ref[...] + gate_ref[...] * acc_ref[...]
        if final_norm:
            ms = jnp.mean(xo * xo, axis=-1, keepdims=True)
            xo = xo * lax.rsqrt(ms + RMS_EPS) * gf_ref[...]
        o_ref[...] = xo


def _ffn(x1, h2, mod4, gf, w1, w3, w2, tm, tf, final_norm):
    nb, t, d = x1.shape
    f = w1.shape[1]
    rows = pl.BlockSpec((None, tm, d), lambda b, i, j: (b, i, 0))
    return pl.pallas_call(
        functools.partial(_ffn_kernel, final_norm=final_norm),
        grid=(nb, t // tm, f // tf),
        in_specs=[rows, rows,
                  pl.BlockSpec((None, None, 1, d), lambda b, i, j: (b, 5, 0, 0)),
                  pl.BlockSpec((1, d), lambda b, i, j: (0, 0)),
                  pl.BlockSpec((d, tf), lambda b, i, j: (0, j)),
                  pl.BlockSpec((d, tf), lambda b, i, j: (0, j)),
                  pl.BlockSpec((tf, d), lambda b, i, j: (j, 0))],
        out_specs=rows,
        out_shape=jax.ShapeDtypeStruct((nb, t, d), F32),
        scratch_shapes=[pltpu.VMEM((tm, d), F32)],
        compiler_params=_cparams(("arbitrary", "arbitrary", "arbitrary")),
        name="ffn",
    )(x1, h2, mod4, gf.reshape(1, d), w1, w3, w2)


def _pad_cols(w, n):
    return jnp.concatenate([w, jnp.zeros(w.shape[:-1] + (n - w.shape[-1],), w.dtype)], axis=-1)


def _tile(n, pref):
    return pref if n % pref == 0 else n


def kernel(x, c, w_ada, b_ada, norm1_g, w_in, tshift_mu, conv_w, conv_b, conv_gn_g, conv_gn_b,
           w0, w_w2, a0, a_w2, g_w2, k_k, k_a, r_k, lnx_g, lnx_b, w_out, norm2_g,
           w_ff1, w_ff3, w_ff2, norm_f_g):
    nb, t, d = x.shape
    depth = w_in.shape[0]
    cw = conv_w.shape[2]
    rw = w0.shape[1]
    n_rest = w_in.shape[2] - 2 * cw
    n_pad = -(-n_rest // LANES) * LANES
    assert rw % GROUP == 0 and cw % GROUP == 0

    for l in range(depth):
        mod, wib = _ada(c, w_ada[l], b_ada[l], w_in[l].T)
        mod4 = mod.reshape(nb, N_MOD, 1, d)
        oc, p, w1b, w3b, w2b, wob = _front(
            x, norm1_g[l], mod4, wib, conv_w[l], conv_b[l], conv_gn_g[l], conv_gn_b[l],
            cw, n_pad, _tile(t, 256), 512,
            cast_weights=((w_ff1[l], 1), (w_ff3[l], 1), (w_ff2[l], 2), (w_out[l], 1)))
        r, k, v, an, b, lw, g, bonus = _prep(p, tshift_mu[l], w0[l], w_w2[l], a0[l], a_w2[l], g_w2[l],
                                             k_k[l], k_a[l], r_k[l].reshape(-1), rw, _tile(t, 512))
        y = _wkv(r, k, v, an, b, lw, _tile(t, 256))
        x, h2 = _outproj(oc, y, bonus, g, lnx_g[l], lnx_b[l], wob.reshape(2, cw, d),
                         x, mod4, norm2_g[l], _tile(t, 512), 512)
        x = _ffn(x, h2, mod4, norm_f_g, w1b, w3b, w2b,
                 _tile(t, 512), 512, final_norm=(l == depth - 1))
    return x
```

```python
import functools
import math

import jax
import jax.numpy as jnp
from jax import lax
from jax.experimental import pallas as pl
from jax.experimental.pallas import tpu as pltpu

F32 = jnp.float32
BF16 = jnp.bfloat16

HEAD_DIM = 64
CONV_KERNEL = 31
RMS_EPS = 1e-6
CONV_GN_EPS = 1e-5
RWKV_GN_EPS = 64e-5
N_MOD = 6

LANES = 128
SUBLANES = 8
BF16_ROWS = 16
MXU_DIM = 256
GROUP = MXU_DIM
CHUNK = 64
HALO = 32
VMEM_LIMIT = 52 * 1024 * 1024


def _cparams(sem):
    return pltpu.CompilerParams(dimension_semantics=sem, vmem_limit_bytes=VMEM_LIMIT)


def _dot(a, b):
    return jnp.dot(a, b, preferred_element_type=F32)


def _dot_nt(a, b):
    return lax.dot_general(a, b, (((1,), (1,)), ((), ())), preferred_element_type=F32)


def _dot_tn(a, b):
    return lax.dot_general(a, b, (((0,), (0,)), ((), ())), preferred_element_type=F32)


def _split2(x):
    hi = x.astype(BF16)
    lo = (x - hi.astype(F32)).astype(BF16)
    return hi, lo


def _sigmoid(x):
    return 1.0 / (1.0 + jnp.exp(-x))


def _head_sum(x, ones_bd):
    outs = [_dot(x[:, g * GROUP:(g + 1) * GROUP].astype(BF16), ones_bd)
            for g in range(x.shape[1] // GROUP)]
    return outs[0] if len(outs) == 1 else jnp.concatenate(outs, axis=1)


def _block_diag_ones():
    r = lax.broadcasted_iota(jnp.int32, (GROUP, GROUP), 0) // HEAD_DIM
    c = lax.broadcasted_iota(jnp.int32, (GROUP, GROUP), 1) // HEAD_DIM
    return jnp.where(r == c, 1.0, 0.0).astype(BF16)


def _ada_kernel(cb_ref, w_ref, b_ref, wit_ref, o_ref, wib_ref):
    n_cols = wit_ref.shape[0]
    for r in range(0, n_cols, LANES):
        hi = min(r + LANES, n_cols)
        wib_ref[:, r:hi] = wit_ref[r:hi, :].T.astype(wib_ref.dtype)
    nb = cb_ref.shape[0]
    tn = w_ref.shape[1]
    for b in range(nb):
        cb = cb_ref[b]
        sc = cb * _sigmoid(cb)
        for j in range(tn // LANES):
            sl = slice(j * LANES, (j + 1) * LANES)
            s = jnp.sum(w_ref[:, sl] * sc, axis=0, keepdims=True)
            o_ref[b:b + 1, sl] = s + b_ref[:, sl]


def _ada(c, w_ada, b_ada, w_in_t):
    nb, d = c.shape
    n = w_ada.shape[1]
    steps = w_in_t.shape[1] // LANES
    tn = n // steps
    assert steps * LANES == w_in_t.shape[1] and tn * steps == n and tn % LANES == 0
    cb = jnp.broadcast_to(c[:, :, None], (nb, d, LANES))
    return pl.pallas_call(
        _ada_kernel,
        grid=(steps,),
        in_specs=[pl.BlockSpec((nb, d, LANES), lambda j: (0, 0, 0)),
                  pl.BlockSpec((d, tn), lambda j: (0, j)),
                  pl.BlockSpec((1, tn), lambda j: (0, j)),
                  pl.BlockSpec((w_in_t.shape[0], LANES), lambda j: (0, j))],
        out_specs=[pl.BlockSpec((nb, tn), lambda j: (0, j)),
                   pl.BlockSpec((LANES, w_in_t.shape[0]), lambda j: (j, 0))],
        out_shape=[jax.ShapeDtypeStruct((nb, n), F32),
                   jax.ShapeDtypeStruct(w_in_t.shape[::-1], BF16)],
        compiler_params=_cparams(("arbitrary",)),
        name="ada",
    )(cb, w_ada, b_ada.reshape(1, n), w_in_t)


def _modulated_norm(x, g, sc, sh):
    ms = jnp.mean(x * x, axis=-1, keepdims=True)
    return (x * lax.rsqrt(ms + RMS_EPS) * g) * (1.0 + sc) + sh


def _front_kernel(x_ref, g_ref, sc_ref, sh_ref, w_ref, cw_ref, cb_ref, gg_ref, gb_ref,
                  wf1_ref, wf3_ref, wf2_ref, wo_ref,
                  oc_ref, p_ref, wf1b_ref, wf3b_ref, wf2b_ref, wob_ref,
                  ush_ref, acc_ref, carry_ref, *, tn):
    tm = x_ref.shape[0]
    cw = oc_ref.shape[1]
    nsh = ush_ref.shape[1]
    n_rest = w_ref.shape[1] - 2 * cw
    first = pl.program_id(1) == 0
    h = _modulated_norm(x_ref[...], g_ref[...], sc_ref[...], sh_ref[...]).astype(BF16)

    @pl.when(first)
    def _():
        carry_ref[...] = jnp.zeros_like(carry_ref)

    ush_ref[0, 0:HALO, :] = carry_ref[...]
    for j in range(cw // tn):
        sl = slice(j * tn, (j + 1) * tn)
        val = _dot(h, w_ref[:, sl])
        gate = _dot(h, w_ref[:, cw + j * tn:cw + (j + 1) * tn])
        ush_ref[0, HALO:HALO + tm, sl] = val * _sigmoid(gate)
    for src, dst in ((wf1_ref, wf1b_ref), (wf3_ref, wf3b_ref), (wf2_ref, wf2b_ref), (wo_ref, wob_ref)):
        dst[...] = src[...].astype(dst.dtype)
    carry_ref[...] = ush_ref[0, tm:tm + HALO, :]
    for phi in range(1, SUBLANES):
        ush_ref[phi, 0:nsh - SUBLANES, :] = ush_ref[0, phi:phi + nsh - SUBLANES, :]

    for lo in range(0, n_rest, tn):
        hi = min(lo + tn, n_rest)
        p_ref[:, lo:hi] = _dot(h, w_ref[:, 2 * cw + lo:2 * cw + hi]).astype(p_ref.dtype)
    if p_ref.shape[1] > n_rest:
        p_ref[:, n_rest:] = jnp.zeros((tm, p_ref.shape[1] - n_rest), p_ref.dtype)

    rb = 64
    base = HALO - (CONV_KERNEL - 1)
    for c in range(cw // LANES):
        sl = slice(c * LANES, (c + 1) * LANES)
        for r in range(tm // rb):
            acc = jnp.zeros((rb, LANES), F32) + cb_ref[:, sl]
            for j in range(CONV_KERNEL):
                q, phi = divmod(base + j, SUBLANES)
                lo = r * rb + q * SUBLANES
                acc = acc + cw_ref[j:j + 1, sl] * ush_ref[phi, lo:lo + rb, sl]
            acc_ref[r * rb:(r + 1) * rb, sl] = acc

    ones_bd = _block_diag_ones()
    u = acc_ref[...]
    mean = _head_sum(u, ones_bd) * (1.0 / HEAD_DIM)
    d = u - mean
    var = _head_sum(d * d, ones_bd) * (1.0 / HEAD_DIM)
    y = d * lax.rsqrt(var + CONV_GN_EPS) * gg_ref[...] + gb_ref[...]
    oc_ref[...] = (y * _sigmoid(y)).astype(oc_ref.dtype)


def _front(x, g, mod4, w, conv_w, conv_b, gn_g, gn_b, cw, n_pad, tm, tn, cast_weights):
    nb, t, d = x.shape
    n = w.shape[1]
    nt = t // tm
    row = lambda v: v.reshape(1, cw)
    vec = pl.BlockSpec((1, cw), lambda b, i: (0, 0))
    cast_specs = []
    for cwt, revisits in cast_weights:
        rows = cwt.shape[0] * revisits // (nb * nt)
        assert rows * nb * nt == cwt.shape[0] * revisits and rows % BF16_ROWS == 0
        cast_specs.append(pl.BlockSpec((rows, cwt.shape[1]),
                                       lambda b, i, rv=revisits: ((b * nt + i) // rv, 0)))
    return pl.pallas_call(
        functools.partial(_front_kernel, tn=tn),
        grid=(nb, t // tm),
        in_specs=[pl.BlockSpec((None, tm, d), lambda b, i: (b, i, 0)),
                  pl.BlockSpec((1, d), lambda b, i: (0, 0)),
                  pl.BlockSpec((None, None, 1, d), lambda b, i: (b, 1, 0, 0)),
                  pl.BlockSpec((None, None, 1, d), lambda b, i: (b, 0, 0, 0)),
                  pl.BlockSpec((d, n), lambda b, i: (0, 0), pipeline_mode=pl.Buffered(1)),
                  pl.BlockSpec((CONV_KERNEL, cw), lambda b, i: (0, 0)), vec, vec, vec] + cast_specs,
        out_specs=[pl.BlockSpec((None, tm, cw), lambda b, i: (b, i, 0)),
                   pl.BlockSpec((None, tm, n_pad), lambda b, i: (b, i, 0))] + cast_specs,
        out_shape=[jax.ShapeDtypeStruct((nb, t, cw), BF16),
                   jax.ShapeDtypeStruct((nb, t, n_pad), BF16)]
        + [jax.ShapeDtypeStruct(cwt.shape, BF16) for cwt, _ in cast_weights],
        scratch_shapes=[pltpu.VMEM((SUBLANES, tm + HALO, cw), F32), pltpu.VMEM((tm, cw), F32),
                        pltpu.VMEM((HALO, cw), F32)],
        compiler_params=_cparams(("arbitrary", "arbitrary")),
        name="front",
    )(x, g.reshape(1, d), mod4, mod4, w, conv_w, row(conv_b), row(gn_g), row(gn_b),
      *[cwt for cwt, _ in cast_weights])


def _prep_kernel(pr_ref, pk_ref, pv_ref, pl_ref, hr_ref, hk_ref, hv_ref, hl_ref,
                 mu_rkv_ref, mu_lora_ref,
                 w0_ref, ww2_ref, a0_ref, aw2_ref, gw2_ref, kk_ref, ka_ref, rk_ref,
                 r_ref, k_ref, v_ref, an_ref, b_ref, lw_ref, g_ref, bonus_ref, *, win_w, win_a, win_g):
    tm, rw = r_ref.shape
    first = pl.program_id(1) == 0
    rows = lax.broadcasted_iota(jnp.int32, (tm, 1), 0)

    def shifted(cur_ref, halo_ref, mu):
        cur = cur_ref[...].astype(F32)
        last = jnp.where(first, 0.0, halo_ref[BF16_ROWS - 1:BF16_ROWS, :].astype(F32))
        prev = jnp.where(rows == 0, last, pltpu.roll(cur, 1, axis=0))
        return cur + (prev - cur) * mu

    r = shifted(pr_ref, hr_ref, mu_rkv_ref[:, :rw])
    k = shifted(pk_ref, hk_ref, mu_rkv_ref[:, rw:2 * rw])
    v = shifted(pv_ref, hv_ref, mu_rkv_ref[:, 2 * rw:])
    ql = shifted(pl_ref, hl_ref, mu_lora_ref[...])
    xw, xa, xg = (ql[:, lo:hi] for lo, hi in (win_w, win_a, win_g))

    z = w0_ref[...] + _dot(jnp.tanh(xw).astype(BF16), ww2_ref[...])
    lw_ref[...] = (-math.exp(-0.5)) * _sigmoid(z)
    a = _sigmoid(a0_ref[...] + _dot(xa.astype(BF16), aw2_ref[...]))
    g_ref[...] = _dot(_sigmoid(xg).astype(BF16), gw2_ref[...]).astype(g_ref.dtype)

    ones_bd = _block_diag_ones()
    kk = k * kk_ref[...]
    ss = _head_sum(kk * kk, ones_bd)
    kkn = kk * lax.rsqrt(jnp.maximum(ss, 1e-24))
    kmod = k * (1.0 + (a - 1.0) * ka_ref[...])
    r_ref[...] = r.astype(r_ref.dtype)
    k_ref[...] = kmod.astype(k_ref.dtype)
    v_ref[...] = v.astype(v_ref.dtype)
    an_ref[...] = (-kkn).astype(an_ref.dtype)
    b_ref[...] = (kkn * a).astype(b_ref.dtype)
    bonus_ref[...] = (_head_sum(r * kmod * rk_ref[...], ones_bd) * v).astype(bonus_ref.dtype)


def _lane_window(lo, hi):
    return lo // LANES * LANES, -(-hi // LANES) * LANES


def _window_weight(w, lo, hi):
    a0, a1 = _lane_window(lo, hi)
    out = jnp.zeros((a1 - a0, w.shape[1]), w.dtype)
    return out.at[lo - a0:hi - a0].set(w)


def _prep(p, mu, w0, w_w2, a0, a_w2, g_w2, k_k, k_a, r_k, rw, tm):
    nb, t, n_pad = p.shape
    nrkv = 3 * rw
    nl = n_pad - nrkv
    dl, al, gl = w_w2.shape[0], a_w2.shape[0], g_w2.shape[0]
    assert nrkv % nl == 0
    r_blk = 0
    lora_blk = nrkv // nl
    wins = (_lane_window(0, dl), _lane_window(dl, dl + al), _lane_window(dl + al, dl + al + gl))
    ww2 = _window_weight(w_w2, 0, dl).astype(BF16)
    aw2 = _window_weight(a_w2, dl, dl + al).astype(BF16)
    gw2 = _window_weight(g_w2, dl + al, dl + al + gl).astype(BF16)
    mu_rkv = mu[:nrkv]
    mu_lora = _pad_cols(mu[nrkv:], nl)
    hb = tm // BF16_ROWS
    row = lambda v_: v_.reshape(1, -1)
    full = lambda shape: pl.BlockSpec(shape, lambda b, i: (0,) * len(shape))
    cur = lambda w, blk: pl.BlockSpec((None, tm, w), lambda b, i: (b, i, blk))
    halo = lambda w, blk: pl.BlockSpec((None, BF16_ROWS, w),
                                       lambda b, i: (b, jnp.maximum(i * hb - 1, 0), blk))
    out_spec = pl.BlockSpec((None, tm, rw), lambda b, i: (b, i, 0))
    bf = jax.ShapeDtypeStruct((nb, t, rw), BF16)
    return pl.pallas_call(
        functools.partial(_prep_kernel, win_w=wins[0], win_a=wins[1], win_g=wins[2]),
        grid=(nb, t // tm),
        in_specs=[cur(rw, r_blk), cur(rw, r_blk + 1), cur(rw, r_blk + 2), cur(nl, lora_blk),
                  halo(rw, r_blk), halo(rw, r_blk + 1), halo(rw, r_blk + 2), halo(nl, lora_blk),
                  full((1, nrkv)), full((1, nl)),
                  full((1, rw)), full(ww2.shape), full((1, rw)), full(aw2.shape),
                  full(gw2.shape), full((1, rw)), full((1, rw)), full((1, rw))],
        out_specs=[out_spec] * 8,
        out_shape=[bf, bf, bf, bf, bf, jax.ShapeDtypeStruct((nb, t, rw), F32), bf, bf],
        compiler_params=_cparams(("arbitrary", "arbitrary")),
        name="prep",
    )(p, p, p, p, p, p, p, p, row(mu_rkv), row(mu_lora), row(w0), ww2, row(a0), aw2, gw2,
      row(k_k), row(k_a), row(r_k))


def _wkv_kernel(r_ref, k_ref, v_ref, an_ref, b_ref, lw_ref, y_ref, s_ref):
    nb, tb, rw = r_ref.shape
    ngroups = rw // GROUP
    reps = GROUP // CHUNK
    streams = [(bi, g) for bi in range(nb) for g in range(ngroups)]

    @pl.when(pl.program_id(0) == 0)
    def _():
        s_ref[...] = jnp.zeros_like(s_ref)

    row = lax.broadcasted_iota(jnp.int32, (GROUP, GROUP), 0)
    col = lax.broadcasted_iota(jnp.int32, (GROUP, GROUP), 1)
    bd = (row // HEAD_DIM) == (col // HEAD_DIM)
    bd16 = (row // HEAD_DIM).astype(F32).astype(BF16) == (col // HEAD_DIM).astype(F32).astype(BF16)
    t_idx = lax.broadcasted_iota(jnp.int32, (CHUNK, GROUP), 0)
    j_idx = lax.broadcasted_iota(jnp.int32, (CHUNK, GROUP), 1) % CHUNK
    tj_strict = t_idx > j_idx
    tj_incl = t_idx >= j_idx
    ident = jnp.where(t_idx == j_idx, 1.0, 0.0)
    tr = lax.broadcasted_iota(jnp.int32, (CHUNK, CHUNK), 0)
    tc = lax.broadcasted_iota(jnp.int32, (CHUNK, CHUNK), 1)
    ltri = jnp.where(tr >= tc, 1.0, 0.0).astype(BF16)
    nlev = int(math.log2(CHUNK)) - 1

    def expand(xc):
        xb = xc.astype(BF16)
        return jnp.where(bd16, jnp.concatenate([xb] * reps, axis=0), jnp.zeros((), BF16))

    def stack(a, b):
        return jnp.concatenate([a, b], axis=0).astype(BF16)

    def chunk_body(c, carry):
        rs = pl.ds(pl.multiple_of(c * CHUNK, CHUNK), CHUNK)
        ns = len(streams)
        vg, ar, kb, vexp, kbe, p_end = [], [], [], [], [], []
        for bi, g in streams:
            ls = slice(g * GROUP, (g + 1) * GROUP)
            lw = lw_ref[bi, rs, ls]
            hi, lo = _split2(lw)
            cum = _dot(ltri, hi) + _dot(ltri, lo)
            tail = cum[CHUNK - 1:CHUNK, :]
            e_neg = jnp.exp(-cum)
            e_tail = jnp.exp(tail - cum)
            k_ = k_ref[bi, rs, ls].astype(F32)
            b_ = b_ref[bi, rs, ls].astype(F32)
            v_ = v_ref[bi, rs, ls].astype(F32)
            vg.append(v_)
            ar.append(stack(an_ref[bi, rs, ls].astype(F32) * jnp.exp(cum - lw),
                            r_ref[bi, rs, ls].astype(F32) * jnp.exp(cum)))
            kb.append(jnp.concatenate([expand(k_ * e_neg), expand(b_ * e_neg)], axis=0))
            vexp.append(expand(v_))
            kbe.append(stack(k_ * e_tail, b_ * e_tail))
            p_end.append(jnp.exp(tail))

        amat = [_dot_nt(ar[i], kb[i]) for i in range(ns)]
        a_kc = [stack(jnp.where(tj_strict, amat[i][:CHUNK, :GROUP], 0.0),
                      jnp.where(tj_incl, amat[i][CHUNK:, :GROUP], 0.0)) for i in range(ns)]
        a_rb = [jnp.where(tj_incl, amat[i][CHUNK:, GROUP:], 0.0).astype(BF16) for i in range(ns)]
        xp = [jnp.where(tj_strict, amat[i][:CHUNK, GROUP:], 0.0) for i in range(ns)]

        t_inv = [ident + xp[i] for i in range(ns)]
        xp = [_dot(xp[i].astype(BF16), expand(xp[i])) for i in range(ns)]
        for lev in range(nlev):
            if lev < nlev - 1:
                both = [_dot(stack(xp[i], t_inv[i]), expand(xp[i])) for i in range(ns)]
                xp = [both[i][:CHUNK] for i in range(ns)]
                t_inv = [t_inv[i] + both[i][CHUNK:] for i in range(ns)]
            else:
                t_inv = [t_inv[i] + _dot(t_inv[i].astype(BF16), expand(xp[i])) for i in range(ns)]

        s_old = [s_ref[i] for i in range(ns)]
        ars = [_dot_nt(ar[i], s_old[i].astype(BF16)) for i in range(ns)]
        akv = [_dot(a_kc[i], vexp[i]) for i in range(ns)]
        z = [ars[i][:CHUNK] + akv[i][:CHUNK] for i in range(ns)]
        u = [_dot(t_inv[i].astype(BF16), expand(z[i])) for i in range(ns)]
        y = [ars[i][CHUNK:] + akv[i][CHUNK:] + _dot(a_rb[i], expand(u[i])) for i in range(ns)]
        ds = [_dot_tn(stack(vg[i], u[i]), kbe[i]) for i in range(ns)]
        for i, (bi, g) in enumerate(streams):
            y_ref[bi, rs, g * GROUP:(g + 1) * GROUP] = y[i]
            s_ref[i] = s_old[i] * p_end[i] + jnp.where(bd, ds[i], 0.0)
        return carry

    lax.fori_loop(0, tb // CHUNK, chunk_body, 0, unroll=True)


def _wkv(r, k, v, an, b, lw, tb):
    nb, t, rw = r.shape
    spec = pl.BlockSpec((nb, tb, rw), lambda i: (0, i, 0))
    return pl.pallas_call(
        _wkv_kernel,
        grid=(t // tb,),
        in_specs=[spec] * 6,
        out_specs=spec,
        out_shape=jax.ShapeDtypeStruct((nb, t, rw), F32),
        scratch_shapes=[pltpu.VMEM((nb * (rw // GROUP), GROUP, GROUP), F32)],
        compiler_params=_cparams(("arbitrary",)),
        name="wkv",
    )(r, k, v, an, b, lw)


def _outproj_kernel(oc_ref, y_ref, bonus_ref, g_ref, lg_ref, lb_ref, wa_ref, wb_ref, x_ref, g1_ref,
                    n2_ref, sc2_ref, sh2_ref, o_ref, h2_ref, *, tn, parts):
    tm, d = o_ref.shape
    nchunks = d // tn
    rows = [slice(k * (tm // parts), (k + 1) * (tm // parts)) for k in range(parts)]
    ones_bd = _block_diag_ones()
    acc = [[_dot(oc_ref[r, :], wa_ref[:, j * tn:(j + 1) * tn]) for j in range(nchunks)] for r in rows]
    orw = []
    for r in rows:
        y = y_ref[r, :]
        mean = _head_sum(y, ones_bd) * (1.0 / HEAD_DIM)
        dv = y - mean
        var = _head_sum(dv * dv, ones_bd) * (1.0 / HEAD_DIM)
        yn = dv * lax.rsqrt(var + RWKV_GN_EPS) * lg_ref[...] + lb_ref[...]
        orw.append(((yn + bonus_ref[r, :].astype(F32)) * g_ref[r, :].astype(F32)).astype(BF16))
    for k, r in enumerate(rows):
        for j in range(nchunks):
            sl = slice(j * tn, (j + 1) * tn)
            o_ref[r, sl] = x_ref[r, sl] + g1_ref[:, sl] * (acc[k][j] + _dot(orw[k], wb_ref[:, sl]))
    for r in rows:
        h2_ref[r, :] = _modulated_norm(o_ref[r, :], n2_ref[...], sc2_ref[...], sh2_ref[...]).astype(BF16)


def _outproj(oc, y, bonus, g, lnx_g, lnx_b, w_out2, x, mod4, norm2_g, tm, tn):
    nb, t, d = x.shape
    cw, rw = oc.shape[2], y.shape[2]
    rows = lambda w: pl.BlockSpec((None, tm, w), lambda b, i: (b, i, 0))
    vec = lambda w: pl.BlockSpec((1, w), lambda b, i: (0, 0))
    modspec = lambda comp: pl.BlockSpec((None, None, 1, d), lambda b, i: (b, comp, 0, 0))
    return pl.pallas_call(
        functools.partial(_outproj_kernel, tn=tn, parts=2),
        grid=(nb, t // tm),
        in_specs=[rows(cw), rows(rw), rows(rw), rows(rw), vec(rw), vec(rw),
                  pl.BlockSpec((None, cw, d), lambda b, i: (0, 0, 0), pipeline_mode=pl.Buffered(1)),
                  pl.BlockSpec((None, rw, d), lambda b, i: (1, 0, 0), pipeline_mode=pl.Buffered(1)),
                  rows(d), modspec(2), vec(d), modspec(4), modspec(3)],
        out_specs=[rows(d), rows(d)],
        out_shape=[jax.ShapeDtypeStruct((nb, t, d), F32), jax.ShapeDtypeStruct((nb, t, d), BF16)],
        compiler_params=_cparams(("arbitrary", "arbitrary")),
        name="outproj",
    )(oc, y, bonus, g, lnx_g.reshape(1, rw), lnx_b.reshape(1, rw), w_out2, w_out2, x, mod4,
      norm2_g.reshape(1, d), mod4, mod4)


def _ffn_kernel(x_ref, h_ref, gate_ref, gf_ref, w1_ref, w3_ref, w2_ref, o_ref, acc_ref, *, final_norm):
    j = pl.program_id(2)

    @pl.when(j == 0)
    def _():
        acc_ref[...] = jnp.zeros_like(acc_ref)

    h = h_ref[...]
    a = _dot(h, w1_ref[...])
    b = _dot(h, w3_ref[...])
    act = (a * _sigmoid(a) * b).astype(BF16)
    acc_ref[...] += _dot(act, w2_ref[...])

    @pl.when(j == pl.num_programs(2) - 1)
    def _():
        xo = x_ref[...] + gate_ref[...] * acc_ref[...]
        if final_norm:
            ms = jnp.mean(xo * xo, axis=-1, keepdims=True)
            xo = xo * lax.rsqrt(ms + RMS_EPS) * gf_ref[...]
        o_ref[...] = xo


def _ffn(x1, h2, mod4, gf, w1, w3, w2, tm, tf, final_norm):
    nb, t, d = x1.shape
    f = w1.shape[1]
    rows = pl.BlockSpec((None, tm, d), lambda b, i, j: (b, i, 0))
    return pl.pallas_call(
        functools.partial(_ffn_kernel, final_norm=final_norm),
        grid=(nb, t // tm, f // tf),
        in_specs=[rows, rows,
                  pl.BlockSpec((None, None, 1, d), lambda b, i, j: (b, 5, 0, 0)),
                  pl.BlockSpec((1, d), lambda b, i, j: (0, 0)),
                  pl.BlockSpec((d, tf), lambda b, i, j: (0, j)),
                  pl.BlockSpec((d, tf), lambda b, i, j: (0, j)),
                  pl.BlockSpec((tf, d), lambda b, i, j: (j, 0))],
        out_specs=rows,
        out_shape=jax.ShapeDtypeStruct((nb, t, d), F32),
        scratch_shapes=[pltpu.VMEM((tm, d), F32)],
        compiler_params=_cparams(("arbitrary", "arbitrary", "arbitrary")),
        name="ffn",
    )(x1, h2, mod4, gf.reshape(1, d), w1, w3, w2)


def _pad_cols(w, n):
    return jnp.concatenate([w, jnp.zeros(w.shape[:-1] + (n - w.shape[-1],), w.dtype)], axis=-1)


def _tile(n, pref):
    return pref if n % pref == 0 else n


def kernel(x, c, w_ada, b_ada, norm1_g, w_in, tshift_mu, conv_w, conv_b, conv_gn_g, conv_gn_b,
           w0, w_w2, a0, a_w2, g_w2, k_k, k_a, r_k, lnx_g, lnx_b, w_out, norm2_g,
           w_ff1, w_ff3, w_ff2, norm_f_g):
    nb, t, d = x.shape
    depth = w_in.shape[0]
    cw = conv_w.shape[2]
    rw = w0.shape[1]
    n_rest = w_in.shape[2] - 2 * cw
    n_pad = -(-n_rest // LANES) * LANES
    assert rw % GROUP == 0 and cw % GROUP == 0

    for l in range(depth):
        mod, wib = _ada(c, w_ada[l], b_ada[l], w_in[l].T)
        mod4 = mod.reshape(nb, N_MOD, 1, d)
        oc, p, w1b, w3b, w2b, wob = _front(
            x, norm1_g[l], mod4, wib, conv_w[l], conv_b[l], conv_gn_g[l], conv_gn_b[l],
            cw, n_pad, _tile(t, 256), 512,
            cast_weights=((w_ff1[l], 1), (w_ff3[l], 1), (w_ff2[l], 2), (w_out[l], 1)))
        r, k, v, an, b, lw, g, bonus = _prep(p, tshift_mu[l], w0[l], w_w2[l], a0[l], a_w2[l], g_w2[l],
                                             k_k[l], k_a[l], r_k[l].reshape(-1), rw, _tile(t, 512))
        y = _wkv(r, k, v, an, b, lw, _tile(t, 512))
        x, h2 = _outproj(oc, y, bonus, g, lnx_g[l], lnx_b[l], wob.reshape(2, cw, d),
                         x, mod4, norm2_g[l], _tile(t, 512), 512)
        x = _ffn(x, h2, mod4, norm_f_g, w1b, w3b, w2b,
                 _tile(t, 512), 512, final_norm=(l == depth - 1))
    return x
```

```python
import functools
import math

import jax
import jax.numpy as jnp
from jax import lax
from jax.experimental import pallas as pl
from jax.experimental.pallas import tpu as pltpu

F32 = jnp.float32
BF16 = jnp.bfloat16

HEAD_DIM = 64
CONV_KERNEL = 31
RMS_EPS = 1e-6
CONV_GN_EPS = 1e-5
RWKV_GN_EPS = 64e-5
N_MOD = 6

LANES = 128
SUBLANES = 8
BF16_ROWS = 16
MXU_DIM = 256
GROUP = MXU_DIM
CHUNK = 64
HALO = 32
VMEM_LIMIT = 52 * 1024 * 1024


def _cparams(sem):
    return pltpu.CompilerParams(dimension_semantics=sem, vmem_limit_bytes=VMEM_LIMIT)


def _dot(a, b):
    return jnp.dot(a, b, preferred_element_type=F32)


def _dot_nt(a, b):
    return lax.dot_general(a, b, (((1,), (1,)), ((), ())), preferred_element_type=F32)


def _dot_tn(a, b):
    return lax.dot_general(a, b, (((0,), (0,)), ((), ())), preferred_element_type=F32)


def _split2(x):
    hi = x.astype(BF16)
    lo = (x - hi.astype(F32)).astype(BF16)
    return hi, lo


def _sigmoid(x):
    return 1.0 / (1.0 + jnp.exp(-x))


def _head_sum(x, ones_bd):
    outs = [_dot(x[:, g * GROUP:(g + 1) * GROUP].astype(BF16), ones_bd)
            for g in range(x.shape[1] // GROUP)]
    return outs[0] if len(outs) == 1 else jnp.concatenate(outs, axis=1)


def _block_diag_ones():
    r = lax.broadcasted_iota(jnp.int32, (GROUP, GROUP), 0) // HEAD_DIM
    c = lax.broadcasted_iota(jnp.int32, (GROUP, GROUP), 1) // HEAD_DIM
    return jnp.where(r == c, 1.0, 0.0).astype(BF16)


def _ada_kernel(cb_ref, w_ref, b_ref, wit_ref, o_ref, wib_ref):
    n_cols = wit_ref.shape[0]
    for r in range(0, n_cols, LANES):
        hi = min(r + LANES, n_cols)
        wib_ref[:, r:hi] = wit_ref[r:hi, :].T.astype(wib_ref.dtype)
    nb = cb_ref.shape[0]
    tn = w_ref.shape[1]
    for b in range(nb):
        cb = cb_ref[b]
        sc = cb * _sigmoid(cb)
        for j in range(tn // LANES):
            sl = slice(j * LANES, (j + 1) * LANES)
            s = jnp.sum(w_ref[:, sl] * sc, axis=0, keepdims=True)
            o_ref[b:b + 1, sl] = s + b_ref[:, sl]


def _ada(c, w_ada, b_ada, w_in_t):
    nb, d = c.shape
    n = w_ada.shape[1]
    steps = w_in_t.shape[1] // LANES
    tn = n // steps
    assert steps * LANES == w_in_t.shape[1] and tn * steps == n and tn % LANES == 0
    cb = jnp.broadcast_to(c[:, :, None], (nb, d, LANES))
    return pl.pallas_call(
        _ada_kernel,
        grid=(steps,),
        in_specs=[pl.BlockSpec((nb, d, LANES), lambda j: (0, 0, 0)),
                  pl.BlockSpec((d, tn), lambda j: (0, j)),
                  pl.BlockSpec((1, tn), lambda j: (0, j)),
                  pl.BlockSpec((w_in_t.shape[0], LANES), lambda j: (0, j))],
        out_specs=[pl.BlockSpec((nb, tn), lambda j: (0, j)),
                   pl.BlockSpec((LANES, w_in_t.shape[0]), lambda j: (j, 0))],
        out_shape=[jax.ShapeDtypeStruct((nb, n), F32),
                   jax.ShapeDtypeStruct(w_in_t.shape[::-1], BF16)],
        compiler_params=_cparams(("arbitrary",)),
        name="ada",
    )(cb, w_ada, b_ada.reshape(1, n), w_in_t)


def _modulated_norm(x, g, sc, sh):
    ms = jnp.mean(x * x, axis=-1, keepdims=True)
    return (x * lax.rsqrt(ms + RMS_EPS) * g) * (1.0 + sc) + sh


def _front_kernel(x_ref, g_ref, sc_ref, sh_ref, w_ref, cw_ref, cb_ref, gg_ref, gb_ref,
                  wf1_ref, wf3_ref, wf2_ref, wo_ref,
                  oc_ref, p_ref, wf1b_ref, wf3b_ref, wf2b_ref, wob_ref,
                  ush_ref, acc_ref, carry_ref, *, tn):
    tm = x_ref.shape[0]
    cw = oc_ref.shape[1]
    nsh = ush_ref.shape[1]
    n_rest = w_ref.shape[1] - 2 * cw
    first = pl.program_id(1) == 0
    h = _modulated_norm(x_ref[...], g_ref[...], sc_ref[...], sh_ref[...]).astype(BF16)

    @pl.when(first)
    def _():
        carry_ref[...] = jnp.zeros_like(carry_ref)

    ush_ref[0, 0:HALO, :] = carry_ref[...]
    for j in range(cw // tn):
        sl = slice(j * tn, (j + 1) * tn)
        val = _dot(h, w_ref[:, sl])
        gate = _dot(h, w_ref[:, cw + j * tn:cw + (j + 1) * tn])
        ush_ref[0, HALO:HALO + tm, sl] = val * _sigmoid(gate)
    for src, dst in ((wf1_ref, wf1b_ref), (wf3_ref, wf3b_ref), (wf2_ref, wf2b_ref), (wo_ref, wob_ref)):
        dst[...] = src[...].astype(dst.dtype)
    carry_ref[...] = ush_ref[0, tm:tm + HALO, :]
    for phi in range(1, SUBLANES):
        ush_ref[phi, 0:nsh - SUBLANES, :] = ush_ref[0, phi:phi + nsh - SUBLANES, :]

    for lo in range(0, n_rest, tn):
        hi = min(lo + tn, n_rest)
        p_ref[:, lo:hi] = _dot(h, w_ref[:, 2 * cw + lo:2 * cw + hi]).astype(p_ref.dtype)
    if p_ref.shape[1] > n_rest:
        p_ref[:, n_rest:] = jnp.zeros((tm, p_ref.shape[1] - n_rest), p_ref.dtype)

    rb = 64
    base = HALO - (CONV_KERNEL - 1)
    for c in range(cw // LANES):
        sl = slice(c * LANES, (c + 1) * LANES)
        for r in range(tm // rb):
            acc = jnp.zeros((rb, LANES), F32) + cb_ref[:, sl]
            for j in range(CONV_KERNEL):
                q, phi = divmod(base + j, SUBLANES)
                lo = r * rb + q * SUBLANES
                acc = acc + cw_ref[j:j + 1, sl] * ush_ref[phi, lo:lo + rb, sl]
            acc_ref[r * rb:(r + 1) * rb, sl] = acc

    ones_bd = _block_diag_ones()
    u = acc_ref[...]
    mean = _head_sum(u, ones_bd) * (1.0 / HEAD_DIM)
    d = u - mean
    var = _head_sum(d * d, ones_bd) * (1.0 / HEAD_DIM)
    y = d * lax.rsqrt(var + CONV_GN_EPS) * gg_ref[...] + gb_ref[...]
    oc_ref[...] = (y * _sigmoid(y)).astype(oc_ref.dtype)


def _front(x, g, mod4, w, conv_w, conv_b, gn_g, gn_b, cw, n_pad, tm, tn, cast_weights):
    nb, t, d = x.shape
    n = w.shape[1]
    nt = t // tm
    row = lambda v: v.reshape(1, cw)
    vec = pl.BlockSpec((1, cw), lambda b, i: (0, 0))
    cast_specs = []
    for cwt, revisits in cast_weights:
        rows = cwt.shape[0] * revisits // (nb * nt)
        assert rows * nb * nt == cwt.shape[0] * revisits and rows % BF16_ROWS == 0
        cast_specs.append(pl.BlockSpec((rows, cwt.shape[1]),
                                       lambda b, i, rv=revisits: ((b * nt + i) // rv, 0)))
    return pl.pallas_call(
        functools.partial(_front_kernel, tn=tn),
        grid=(nb, t // tm),
        in_specs=[pl.BlockSpec((None, tm, d), lambda b, i: (b, i, 0)),
                  pl.BlockSpec((1, d), lambda b, i: (0, 0)),
                  pl.BlockSpec((None, None, 1, d), lambda b, i: (b, 1, 0, 0)),
                  pl.BlockSpec((None, None, 1, d), lambda b, i: (b, 0, 0, 0)),
                  pl.BlockSpec((d, n), lambda b, i: (0, 0), pipeline_mode=pl.Buffered(1)),
                  pl.BlockSpec((CONV_KERNEL, cw), lambda b, i: (0, 0)), vec, vec, vec] + cast_specs,
        out_specs=[pl.BlockSpec((None, tm, cw), lambda b, i: (b, i, 0)),
                   pl.BlockSpec((None, tm, n_pad), lambda b, i: (b, i, 0))] + cast_specs,
        out_shape=[jax.ShapeDtypeStruct((nb, t, cw), BF16),
                   jax.ShapeDtypeStruct((nb, t, n_pad), BF16)]
        + [jax.ShapeDtypeStruct(cwt.shape, BF16) for cwt, _ in cast_weights],
        scratch_shapes=[pltpu.VMEM((SUBLANES, tm + HALO, cw), F32), pltpu.VMEM((tm, cw), F32),
                        pltpu.VMEM((HALO, cw), F32)],
        compiler_params=_cparams(("arbitrary", "arbitrary")),
        name="front",
    )(x, g.reshape(1, d), mod4, mod4, w, conv_w, row(conv_b), row(gn_g), row(gn_b),
      *[cwt for cwt, _ in cast_weights])


def _prep_kernel(pr_ref, pk_ref, pv_ref, pl_ref, hr_ref, hk_ref, hv_ref, hl_ref,
                 mu_rkv_ref, mu_lora_ref,
                 w0_ref, ww2_ref, a0_ref, aw2_ref, gw2_ref, kk_ref, ka_ref, rk_ref,
                 r_ref, k_ref, v_ref, an_ref, b_ref, lw_ref, g_ref, bonus_ref, *, win_w, win_a, win_g):
    tm, rw = r_ref.shape
    first = pl.program_id(1) == 0
    rows = lax.broadcasted_iota(jnp.int32, (tm, 1), 0)

    def shifted(cur_ref, halo_ref, mu):
        cur = cur_ref[...].astype(F32)
        last = jnp.where(first, 0.0, halo_ref[BF16_ROWS - 1:BF16_ROWS, :].astype(F32))
        prev = jnp.where(rows == 0, last, pltpu.roll(cur, 1, axis=0))
        return cur + (prev - cur) * mu

    r = shifted(pr_ref, hr_ref, mu_rkv_ref[:, :rw])
    k = shifted(pk_ref, hk_ref, mu_rkv_ref[:, rw:2 * rw])
    v = shifted(pv_ref, hv_ref, mu_rkv_ref[:, 2 * rw:])
    ql = shifted(pl_ref, hl_ref, mu_lora_ref[...])
    xw, xa, xg = (ql[:, lo:hi] for lo, hi in (win_w, win_a, win_g))

    z = w0_ref[...] + _dot(jnp.tanh(xw).astype(BF16), ww2_ref[...])
    lw_ref[...] = (-math.exp(-0.5)) * _sigmoid(z)
    a = _sigmoid(a0_ref[...] + _dot(xa.astype(BF16), aw2_ref[...]))
    g_ref[...] = _dot(_sigmoid(xg).astype(BF16), gw2_ref[...]).astype(g_ref.dtype)

    ones_bd = _block_diag_ones()
    kk = k * kk_ref[...]
    ss = _head_sum(kk * kk, ones_bd)
    kkn = kk * lax.rsqrt(jnp.maximum(ss, 1e-24))
    kmod = k * (1.0 + (a - 1.0) * ka_ref[...])
    r_ref[...] = r.astype(r_ref.dtype)
    k_ref[...] = kmod.astype(k_ref.dtype)
    v_ref[...] = v.astype(v_ref.dtype)
    an_ref[...] = (-kkn).astype(an_ref.dtype)
    b_ref[...] = (kkn * a).astype(b_ref.dtype)
    bonus_ref[...] = (_head_sum(r * kmod * rk_ref[...], ones_bd) * v).astype(bonus_ref.dtype)


def _lane_window(lo, hi):
    return lo // LANES * LANES, -(-hi // LANES) * LANES


def _window_weight(w, lo, hi):
    a0, a1 = _lane_window(lo, hi)
    out = jnp.zeros((a1 - a0, w.shape[1]), w.dtype)
    return out.at[lo - a0:hi - a0].set(w)


def _prep(p, mu, w0, w_w2, a0, a_w2, g_w2, k_k, k_a, r_k, rw, tm):
    nb, t, n_pad = p.shape
    nrkv = 3 * rw
    nl = n_pad - nrkv
    dl, al, gl = w_w2.shape[0], a_w2.shape[0], g_w2.shape[0]
    assert nrkv % nl == 0
    r_blk = 0
    lora_blk = nrkv // nl
    wins = (_lane_window(0, dl), _lane_window(dl, dl + al), _lane_window(dl + al, dl + al + gl))
    ww2 = _window_weight(w_w2, 0, dl).astype(BF16)
    aw2 = _window_weight(a_w2, dl, dl + al).astype(BF16)
    gw2 = _window_weight(g_w2, dl + al, dl + al + gl).astype(BF16)
    mu_rkv = mu[:nrkv]
    mu_lora = _pad_cols(mu[nrkv:], nl)
    hb = tm // BF16_ROWS
    row = lambda v_: v_.reshape(1, -1)
    full = lambda shape: pl.BlockSpec(shape, lambda b, i: (0,) * len(shape))
    cur = lambda w, blk: pl.BlockSpec((None, tm, w), lambda b, i: (b, i, blk))
    halo = lambda w, blk: pl.BlockSpec((None, BF16_ROWS, w),
                                       lambda b, i: (b, jnp.maximum(i * hb - 1, 0), blk))
    out_spec = pl.BlockSpec((None, tm, rw), lambda b, i: (b, i, 0))
    bf = jax.ShapeDtypeStruct((nb, t, rw), BF16)
    return pl.pallas_call(
        functools.partial(_prep_kernel, win_w=wins[0], win_a=wins[1], win_g=wins[2]),
        grid=(nb, t // tm),
        in_specs=[cur(rw, r_blk), cur(rw, r_blk + 1), cur(rw, r_blk + 2), cur(nl, lora_blk),
                  halo(rw, r_blk), halo(rw, r_blk + 1), halo(rw, r_blk + 2), halo(nl, lora_blk),
                  full((1, nrkv)), full((1, nl)),
                  full((1, rw)), full(ww2.shape), full((1, rw)), full(aw2.shape),
                  full(gw2.shape), full((1, rw)), full((1, rw)), full((1, rw))],
        out_specs=[out_spec] * 8,
        out_shape=[bf, bf, bf, bf, bf, jax.ShapeDtypeStruct((nb, t, rw), F32), bf, bf],
        compiler_params=_cparams(("arbitrary", "arbitrary")),
        name="prep",
    )(p, p, p, p, p, p, p, p, row(mu_rkv), row(mu_lora), row(w0), ww2, row(a0), aw2, gw2,
      row(k_k), row(k_a), row(r_k))


def _wkv_kernel(r_ref, k_ref, v_ref, an_ref, b_ref, lw_ref, y_ref, s_ref):
    nb, tb, rw = r_ref.shape
    ngroups = rw // GROUP
    reps = GROUP // CHUNK
    streams = [(bi, g) for bi in range(nb) for g in range(ngroups)]

    @pl.when(pl.program_id(0) == 0)
    def _():
        s_ref[...] = jnp.zeros_like(s_ref)

    row = lax.broadcasted_iota(jnp.int32, (GROUP, GROUP), 0)
    col = lax.broadcasted_iota(jnp.int32, (GROUP, GROUP), 1)
    bd = (row // HEAD_DIM) == (col // HEAD_DIM)
    bd16 = (row // HEAD_DIM).astype(F32).astype(BF16) == (col // HEAD_DIM).astype(F32).astype(BF16)
    t_idx = lax.broadcasted_iota(jnp.int32, (CHUNK, GROUP), 0)
    j_idx = lax.broadcasted_iota(jnp.int32, (CHUNK, GROUP), 1) % CHUNK
    tj_strict = t_idx > j_idx
    tj_incl = t_idx >= j_idx
    ident = jnp.where(t_idx == j_idx, 1.0, 0.0)
    tr = lax.broadcasted_iota(jnp.int32, (CHUNK, CHUNK), 0)
    tc = lax.broadcasted_iota(jnp.int32, (CHUNK, CHUNK), 1)
    ltri = jnp.where(tr >= tc, 1.0, 0.0).astype(BF16)
    nlev = int(math.log2(CHUNK)) - 1

    def expand(xc):
        xb = xc.astype(BF16)
        return jnp.where(bd16, jnp.concatenate([xb] * reps, axis=0), jnp.zeros((), BF16))

    def stack(a, b):
        return jnp.concatenate([a, b], axis=0).astype(BF16)

    def chunk_body(c, carry):
        rs = pl.ds(pl.multiple_of(c * CHUNK, CHUNK), CHUNK)
        ns = len(streams)
        vg, ar, kb, vexp, kbe, p_end = [], [], [], [], [], []
        for bi, g in streams:
            ls = slice(g * GROUP, (g + 1) * GROUP)
            lw = lw_ref[bi, rs, ls]
            hi, lo = _split2(lw)
            cum = _dot(ltri, hi) + _dot(ltri, lo)
            tail = cum[CHUNK - 1:CHUNK, :]
            e_neg = jnp.exp(-cum)
            e_tail = jnp.exp(tail - cum)
            k_ = k_ref[bi, rs, ls].astype(F32)
            b_ = b_ref[bi, rs, ls].astype(F32)
            v_ = v_ref[bi, rs, ls].astype(F32)
            vg.append(v_)
            ar.append(stack(an_ref[bi, rs, ls].astype(F32) * jnp.exp(cum - lw),
                            r_ref[bi, rs, ls].astype(F32) * jnp.exp(cum)))
            kb.append(jnp.concatenate([expand(k_ * e_neg), expand(b_ * e_neg)], axis=0))
            vexp.append(expand(v_))
            kbe.append(stack(k_ * e_tail, b_ * e_tail))
            p_end.append(jnp.exp(tail))

        amat = [_dot_nt(ar[i], kb[i]) for i in range(ns)]
        a_kc = [stack(jnp.where(tj_strict, amat[i][:CHUNK, :GROUP], 0.0),
                      jnp.where(tj_incl, amat[i][CHUNK:, :GROUP], 0.0)) for i in range(ns)]
        a_rb = [jnp.where(tj_incl, amat[i][CHUNK:, GROUP:], 0.0).astype(BF16) for i in range(ns)]
        xp = [jnp.where(tj_strict, amat[i][:CHUNK, GROUP:], 0.0) for i in range(ns)]

        t_inv = [ident + xp[i] for i in range(ns)]
        xp = [_dot(xp[i].astype(BF16), expand(xp[i])) for i in range(ns)]
        for lev in range(nlev):
            if lev < nlev - 1:
                both = [_dot(stack(xp[i], t_inv[i]), expand(xp[i])) for i in range(ns)]
                xp = [both[i][:CHUNK] for i in range(ns)]
                t_inv = [t_inv[i] + both[i][CHUNK:] for i in range(ns)]
            else:
                t_inv = [t_inv[i] + _dot(t_inv[i].astype(BF16), expand(xp[i])) for i in range(ns)]

        s_old = [s_ref[i] for i in range(ns)]
        ars = [_dot_nt(ar[i], s_old[i].astype(BF16)) for i in range(ns)]
        akv = [_dot(a_kc[i], vexp[i]) for i in range(ns)]
        z = [ars[i][:CHUNK] + akv[i][:CHUNK] for i in range(ns)]
        u = [_dot(t_inv[i].astype(BF16), expand(z[i])) for i in range(ns)]
        y = [ars[i][CHUNK:] + akv[i][CHUNK:] + _dot(a_rb[i], expand(u[i])) for i in range(ns)]
        ds = [_dot_tn(stack(vg[i], u[i]), kbe[i]) for i in range(ns)]
        for i, (bi, g) in enumerate(streams):
            y_ref[bi, rs, g * GROUP:(g + 1) * GROUP] = y[i]
            s_ref[i] = s_old[i] * p_end[i] + jnp.where(bd, ds[i], 0.0)
        return carry

    lax.fori_loop(0, tb // CHUNK, chunk_body, 0, unroll=2)


def _wkv(r, k, v, an, b, lw, tb):
    nb, t, rw = r.shape
    spec = pl.BlockSpec((nb, tb, rw), lambda i: (0, i, 0))
    return pl.pallas_call(
        _wkv_kernel,
        grid=(t // tb,),
        in_specs=[spec] * 6,
        out_specs=spec,
        out_shape=jax.ShapeDtypeStruct((nb, t, rw), F32),
        scratch_shapes=[pltpu.VMEM((nb * (rw // GROUP), GROUP, GROUP), F32)],
        compiler_params=_cparams(("arbitrary",)),
        name="wkv",
    )(r, k, v, an, b, lw)


def _outproj_kernel(oc_ref, y_ref, bonus_ref, g_ref, lg_ref, lb_ref, wa_ref, wb_ref, x_ref, g1_ref,
                    n2_ref, sc2_ref, sh2_ref, o_ref, h2_ref, *, tn, parts):
    tm, d = o_ref.shape
    nchunks = d // tn
    rows = [slice(k * (tm // parts), (k + 1) * (tm // parts)) for k in range(parts)]
    ones_bd = _block_diag_ones()
    acc = [[_dot(oc_ref[r, :], wa_ref[:, j * tn:(j + 1) * tn]) for j in range(nchunks)] for r in rows]
    orw = []
    for r in rows:
        y = y_ref[r, :]
        mean = _head_sum(y, ones_bd) * (1.0 / HEAD_DIM)
        dv = y - mean
        var = _head_sum(dv * dv, ones_bd) * (1.0 / HEAD_DIM)
        yn = dv * lax.rsqrt(var + RWKV_GN_EPS) * lg_ref[...] + lb_ref[...]
        orw.append(((yn + bonus_ref[r, :].astype(F32)) * g_ref[r, :].astype(F32)).astype(BF16))
    for k, r in enumerate(rows):
        for j in range(nchunks):
            sl = slice(j * tn, (j + 1) * tn)
            o_ref[r, sl] = x_ref[r, sl] + g1_ref[:, sl] * (acc[k][j] + _dot(orw[k], wb_ref[:, sl]))
    for r in rows:
        h2_ref[r, :] = _modulated_norm(o_ref[r, :], n2_ref[...], sc2_ref[...], sh2_ref[...]).astype(BF16)


def _outproj(oc, y, bonus, g, lnx_g, lnx_b, w_out2, x, mod4, norm2_g, tm, tn):
    nb, t, d = x.shape
    cw, rw = oc.shape[2], y.shape[2]
    rows = lambda w: pl.BlockSpec((None, tm, w), lambda b, i: (b, i, 0))
    vec = lambda w: pl.BlockSpec((1, w), lambda b, i: (0, 0))
    modspec = lambda comp: pl.BlockSpec((None, None, 1, d), lambda b, i: (b, comp, 0, 0))
    return pl.pallas_call(
        functools.partial(_outproj_kernel, tn=tn, parts=2),
        grid=(nb, t // tm),
        in_specs=[rows(cw), rows(rw), rows(rw), rows(rw), vec(rw), vec(rw),
                  pl.BlockSpec((None, cw, d), lambda b, i: (0, 0, 0), pipeline_mode=pl.Buffered(1)),
                  pl.BlockSpec((None, rw, d), lambda b, i: (1, 0, 0), pipeline_mode=pl.Buffered(1)),
                  rows(d), modspec(2), vec(d), modspec(4), modspec(3)],
        out_specs=[rows(d), rows(d)],
        out_shape=[jax.ShapeDtypeStruct((nb, t, d), F32), jax.ShapeDtypeStruct((nb, t, d), BF16)],
        compiler_params=_cparams(("arbitrary", "arbitrary")),
        name="outproj",
    )(oc, y, bonus, g, lnx_g.reshape(1, rw), lnx_b.reshape(1, rw), w_out2, w_out2, x, mod4,
      norm2_g.reshape(1, d), mod4, mod4)


def _ffn_kernel(x_ref, h_ref, gate_ref, gf_ref, w1_ref, w3_ref, w2_ref, o_ref, acc_ref, *, final_norm):
    j = pl.program_id(2)

    @pl.when(j == 0)
    def _():
        acc_ref[...] = jnp.zeros_like(acc_ref)

    h = h_ref[...]
    a = _dot(h, w1_ref[...])
    b = _dot(h, w3_ref[...])
    act = (a * _sigmoid(a) * b).astype(BF16)
    acc_ref[...] += _dot(act, w2_ref[...])

    @pl.when(j == pl.num_programs(2) - 1)
    def _():
        xo = x_ref[...] + gate_ref[...] * acc_ref[...]
        if final_norm:
            ms = jnp.mean(xo * xo, axis=-1, keepdims=True)
            xo = xo * lax.rsqrt(ms + RMS_EPS) * gf_ref[...]
        o_ref[...] = xo


def _ffn(x1, h2, mod4, gf, w1, w3, w2, tm, tf, final_norm):
    nb, t, d = x1.shape
    f = w1.shape[1]
    rows = pl.BlockSpec((None, tm, d), lambda b, i, j: (b, i, 0))
    return pl.pallas_call(
        functools.partial(_ffn_kernel, final_norm=final_norm),
        grid=(nb, t // tm, f // tf),
        in_specs=[rows, rows,
                  pl.BlockSpec((None, None, 1, d), lambda b, i, j: (b, 5, 0, 0)),
                  pl.BlockSpec((1, d), lambda b, i, j: (0, 0)),
                  pl.BlockSpec((d, tf), lambda b, i, j: (0, j)),
                  pl.BlockSpec((d, tf), lambda b, i, j: (0, j)),
                  pl.BlockSpec((tf, d), lambda b, i, j: (j, 0))],
        out_specs=rows,
        out_shape=jax.ShapeDtypeStruct((nb, t, d), F32),
        scratch_shapes=[pltpu.VMEM((tm, d), F32)],
        compiler_params=_cparams(("arbitrary", "arbitrary", "arbitrary")),
        name="ffn",
    )(x1, h2, mod4, gf.reshape(1, d), w1, w3, w2)


def _pad_cols(w, n):
    return jnp.concatenate([w, jnp.zeros(w.shape[:-1] + (n - w.shape[-1],), w.dtype)], axis=-1)


def _tile(n, pref):
    return pref if n % pref == 0 else n


def kernel(x, c, w_ada, b_ada, norm1_g, w_in, tshift_mu, conv_w, conv_b, conv_gn_g, conv_gn_b,
           w0, w_w2, a0, a_w2, g_w2, k_k, k_a, r_k, lnx_g, lnx_b, w_out, norm2_g,
           w_ff1, w_ff3, w_ff2, norm_f_g):
    nb, t, d = x.shape
    depth = w_in.shape[0]
    cw = conv_w.shape[2]
    rw = w0.shape[1]
    n_rest = w_in.shape[2] - 2 * cw
    n_pad = -(-n_rest // LANES) * LANES
    assert rw % GROUP == 0 and cw % GROUP == 0

    for l in range(depth):
        mod, wib = _ada(c, w_ada[l], b_ada[l], w_in[l].T)
        mod4 = mod.reshape(nb, N_MOD, 1, d)
        oc, p, w1b, w3b, w2b, wob = _front(
            x, norm1_g[l], mod4, wib, conv_w[l], conv_b[l], conv_gn_g[l], conv_gn_b[l],
            cw, n_pad, _tile(t, 256), 512,
            cast_weights=((w_ff1[l], 1), (w_ff3[l], 1), (w_ff2[l], 2), (w_out[l], 1)))
        r, k, v, an, b, lw, g, bonus = _prep(p, tshift_mu[l], w0[l], w_w2[l], a0[l], a_w2[l], g_w2[l],
                                             k_k[l], k_a[l], r_k[l].reshape(-1), rw, _tile(t, 512))
        y = _wkv(r, k, v, an, b, lw, _tile(t, 256))
        x, h2 = _outproj(oc, y, bonus, g, lnx_g[l], lnx_b[l], wob.reshape(2, cw, d),
                         x, mod4, norm2_g[l], _tile(t, 512), 512)
        x = _ffn(x, h2, mod4, norm_f_g, w1b, w3b, w2b,
                 _tile(t, 512), 512, final_norm=(l == depth - 1))
    return x
```
